```python
import functools
import jax, jax.numpy as jnp
from jax import lax
import numpy as np

D_MODEL = 2048
BATCH = 8
SEQ = 2048
DEPTH = 1
DEC_BATCH = 32
DEC_SEQ = 16
PAST_LEN = 4096

CHUNK = 64
HEAD_DIM = 128
H_FOX = 8
H_BAND = 8
W_FOX = H_FOX * HEAD_DIM
W_BAND = H_BAND * HEAD_DIM
LEFT_CHUNKS = 8
BAND_CHUNKS = LEFT_CHUNKS + 1
BAND_WINDOW = LEFT_CHUNKS * CHUNK
REL_MAX = 128
Q_BLOCK = 128
N_KEYS = 128
N_EXPERTS = N_KEYS * N_KEYS
PEER_HEADS = 8
PEER_TOPK = 16
PEER_QDIM = 256
PEER_HALF = PEER_QDIM // 2
PEER_BLOCK = 128
EPS = 1e-6
FORGET_BIAS = 3.0
ADA_SCALE = 0.5
C_IN = 3 * W_FOX + H_FOX + 3 * W_BAND + 2 * D_MODEL

kernel_name = 'fox_band_peer_adaln_streaming_step'


def _rmsnorm(x, g):
    xf = x.astype(jnp.float32)
    y = xf * lax.rsqrt(jnp.mean(xf * xf, axis=-1, keepdims=True) + EPS)
    return (y * g.astype(jnp.float32)).astype(x.dtype)


def _adaln(c, w_ada, b_ada):
    h = jax.nn.silu(c) @ w_ada + b_ada
    return jnp.split(h[:, None, :], 6, axis=-1)


def _mixer_inputs(xm, w_in, b_forget):
    B, T, _ = xm.shape
    h = xm @ w_in
    sizes = (W_FOX, W_FOX, W_FOX, H_FOX, W_BAND, W_BAND, W_BAND, D_MODEL, D_MODEL)
    parts, o = [], 0
    for s in sizes:
        parts.append(h[..., o:o + s])
        o += s
    fq, fk, fv, ff, bq, bk, bv, ga, gb = parts
    hd = lambda t, n: t.reshape(B, T, n, HEAD_DIM)
    logf = jax.nn.log_sigmoid(ff.astype(jnp.float32) + b_forget.astype(jnp.float32))
    return (hd(fq, H_FOX), hd(fk, H_FOX), hd(fv, H_FOX), logf,
            hd(bq, H_BAND), hd(bk, H_BAND), hd(bv, H_BAND), ga, gb)


def _fox_attend(q, k, v, Fq, Fk, pos_q, pos_k):
    logits = jnp.einsum('bqhd,bkhd->bhqk', q, k).astype(jnp.float32) * (HEAD_DIM ** -0.5)
    decay = Fq[..., :, None] - Fk[..., None, :]
    logits = jnp.where(pos_k[None, :] <= pos_q[:, None], logits + decay, -jnp.inf)
    p = jax.nn.softmax(logits, axis=-1).astype(v.dtype)
    return jnp.einsum('bhqk,bkhd->bqhd', p, v)


def _fox_prompt(q, k, v, logf):
    B, S, H, _ = q.shape
    F = jnp.transpose(jnp.cumsum(logf, axis=1), (0, 2, 1))
    nb = S // Q_BLOCK
    qb = jnp.moveaxis(q.reshape(B, nb, Q_BLOCK, H, HEAD_DIM), 1, 0)
    Fb = jnp.moveaxis(F.reshape(B, H, nb, Q_BLOCK), 2, 0)
    pos_k = jnp.arange(S)

    def one_block(args):
        qi, Fi, i = args
        return _fox_attend(qi, k, v, Fi, F, i * Q_BLOCK + jnp.arange(Q_BLOCK), pos_k)

    y = lax.map(one_block, (qb, Fb, jnp.arange(nb)))
    y = jnp.moveaxis(y, 0, 1).reshape(B, S, H * HEAD_DIM)
    return y, (k, v, logf)


def _fox_sample(q, k, v, logf, cache_k, cache_v, cache_logf):
    B, T, H, _ = q.shape
    P = cache_k.shape[1]
    k_all = jnp.concatenate([cache_k.astype(k.dtype), k], axis=1)
    v_all = jnp.concatenate([cache_v.astype(v.dtype), v], axis=1)
    lf_all = jnp.concatenate([cache_logf.astype(jnp.float32), logf], axis=1)
    F = jnp.transpose(jnp.cumsum(lf_all, axis=1), (0, 2, 1))
    y = _fox_attend(q, k_all, v_all, F[..., P:], F, P + jnp.arange(T), jnp.arange(P + T))
    return y.reshape(B, T, H * HEAD_DIM), (k, v, logf)


def _rel_bias(d, rel_bias):
    return rel_bias[:, jnp.clip(d, -REL_MAX, REL_MAX) + REL_MAX].astype(jnp.float32)


def _band_prompt(q, k, v, rel_bias):
    B, S, H, _ = q.shape
    nc = S // CHUNK
    L = BAND_CHUNKS * CHUNK
    idx = jnp.arange(nc)[:, None] + jnp.arange(BAND_CHUNKS)[None, :]

    def band(t):
        tc = t.reshape(B, nc, CHUNK, H, HEAD_DIM)
        tc = jnp.pad(tc, ((0, 0), (LEFT_CHUNKS, 0), (0, 0), (0, 0), (0, 0)))
        return tc[:, idx].reshape(B, nc, L, H, HEAD_DIM)

    kb, vb = band(k), band(v)
    qc = q.reshape(B, nc, CHUNK, H, HEAD_DIM)
    logits = jnp.einsum('bnqhd,bnkhd->bhnqk', qc, kb).astype(jnp.float32) * (HEAD_DIM ** -0.5)
    d = BAND_WINDOW + jnp.arange(CHUNK)[:, None] - jnp.arange(L)[None, :]
    bias = _rel_bias(d, rel_bias)
    key_chunk = jnp.arange(nc)[:, None] - LEFT_CHUNKS + jnp.arange(BAND_CHUNKS)[None, :]
    valid = jnp.repeat(key_chunk >= 0, CHUNK, axis=1)
    logits = jnp.where(valid[None, None, :, None, :], logits + bias[None, :, None], -jnp.inf)
    p = jax.nn.softmax(logits, axis=-1).astype(vb.dtype)
    y = jnp.einsum('bhnqk,bnkhd->bnqhd', p, vb).reshape(B, S, H * HEAD_DIM)
    W = min(BAND_WINDOW, S)
    return y, (k[:, S - W:], v[:, S - W:])


def _band_sample(q, k, v, cache_k, cache_v, rel_bias):
    B, T, H, _ = q.shape
    W = cache_k.shape[1]
    k_all = jnp.concatenate([cache_k.astype(k.dtype), k], axis=1)
    v_all = jnp.concatenate([cache_v.astype(v.dtype), v], axis=1)
    d = W + jnp.arange(T)[:, None] - jnp.arange(W + T)[None, :]
    logits = jnp.einsum('bqhd,bkhd->bhqk', q, k_all).astype(jnp.float32) * (HEAD_DIM ** -0.5)
    logits = logits + _rel_bias(d, rel_bias)[None]
    p = jax.nn.softmax(logits, axis=-1).astype(v_all.dtype)
    y = jnp.einsum('bhqk,bkhd->bqhd', p, v_all)
    return y.reshape(B, T, H * HEAD_DIM), (k, v)


def _peer(x, w_query, sub_keys, expert_u, expert_v):
    B, T, D = x.shape
    n = B * T
    nb = -(-n // PEER_BLOCK)
    xt = jnp.pad(x.reshape(n, D), ((0, nb * PEER_BLOCK - n), (0, 0)))

    def one_block(xb):
        q = (xb @ w_query).reshape(PEER_BLOCK, PEER_HEADS, 2, PEER_HALF)
        s = jnp.einsum('thcd,hcnd->thcn', q, sub_keys).astype(jnp.float32)
        s1, i1 = lax.top_k(s[:, :, 0], PEER_TOPK)
        s2, i2 = lax.top_k(s[:, :, 1], PEER_TOPK)
        cand = (s1[..., :, None] + s2[..., None, :]).reshape(PEER_BLOCK, PEER_HEADS, PEER_TOPK * PEER_TOPK)
        top_s, top_c = lax.top_k(cand, PEER_TOPK)
        e = (jnp.take_along_axis(i1, top_c // PEER_TOPK, axis=-1) * N_KEYS
             + jnp.take_along_axis(i2, top_c % PEER_TOPK, axis=-1))
        g = jax.nn.softmax(top_s, axis=-1).astype(xb.dtype)
        h = jax.nn.gelu(jnp.einsum('thkd,td->thk', expert_u[e], xb), approximate=False)
        return jnp.einsum('thk,thkd->td', g * h, expert_v[e])

    y = lax.map(one_block, xt.reshape(nb, PEER_BLOCK, D))
    return y.reshape(nb * PEER_BLOCK, D)[:n].reshape(B, T, D)


def _block(x, c, fox_fn, band_fn, w_ada, b_ada, norm_mix, norm_ffn, w_in, b_forget,
           w_branch_fox, w_branch_band, w_out, w_query, sub_keys, expert_u, expert_v):
    sh1, sc1, g1, sh2, sc2, g2 = _adaln(c, w_ada, b_ada)
    xm = _rmsnorm(x, norm_mix) * (1 + sc1) + sh1
    fq, fk, fv, logf, bq, bk, bv, ga, gb = _mixer_inputs(xm, w_in, b_forget)
    ya, fox_state = fox_fn(fq, fk, fv, logf)
    yb, band_state = band_fn(bq, bk, bv)
    merged = jax.nn.sigmoid(ga) * (ya @ w_branch_fox) + jax.nn.sigmoid(gb) * (yb @ w_branch_band)
    x = x + g1 * (merged @ w_out)
    xf = _rmsnorm(x, norm_ffn) * (1 + sc2) + sh2
    x = x + g2 * _peer(xf, w_query, sub_keys, expert_u, expert_v)
    return x, fox_state + band_state


def setup_inputs(seed: int = 0) -> dict:
    key = jax.random.key(seed)
    ks = jax.random.split(key, 24)
    nrm = lambda k, shape, s: jax.random.normal(k, shape, jnp.float32) * s
    band_len = min(BAND_WINDOW, PAST_LEN)
    return {
        'x_prompt': nrm(ks[0], (BATCH, SEQ, D_MODEL), 1.0),
        'x_sample': nrm(ks[1], (DEC_BATCH, DEC_SEQ, D_MODEL), 1.0),
        'c_prompt': nrm(ks[2], (BATCH, D_MODEL), 1.0),
        'c_sample': nrm(ks[3], (DEC_BATCH, D_MODEL), 1.0),
        'cache_fox_k': nrm(ks[4], (DEPTH, DEC_BATCH, PAST_LEN, H_FOX, HEAD_DIM), 1.0),
        'cache_fox_v': nrm(ks[5], (DEPTH, DEC_BATCH, PAST_LEN, H_FOX, HEAD_DIM), 1.0),
        'cache_fox_logf': jax.nn.log_sigmoid(FORGET_BIAS + nrm(ks[6], (DEPTH, DEC_BATCH, PAST_LEN, H_FOX), 1.0)),
        'cache_band_k': nrm(ks[7], (DEPTH, DEC_BATCH, band_len, H_BAND, HEAD_DIM), 1.0),
        'cache_band_v': nrm(ks[8], (DEPTH, DEC_BATCH, band_len, H_BAND, HEAD_DIM), 1.0),
        'w_ada': nrm(ks[9], (DEPTH, D_MODEL, 6 * D_MODEL), ADA_SCALE * D_MODEL ** -0.5),
        'b_ada': nrm(ks[10], (DEPTH, 6 * D_MODEL), 0.01),
        'norm_mix': 1.0 + nrm(ks[11], (DEPTH, D_MODEL), 0.01),
        'norm_ffn': 1.0 + nrm(ks[12], (DEPTH, D_MODEL), 0.01),
        'w_in': nrm(ks[13], (DEPTH, D_MODEL, C_IN), D_MODEL ** -0.5),
        'b_forget': FORGET_BIAS + nrm(ks[14], (DEPTH, H_FOX), 0.5),
        'rel_bias': nrm(ks[15], (DEPTH, H_BAND, 2 * REL_MAX + 1), 0.5),
        'w_branch_fox': nrm(ks[16], (DEPTH, W_FOX, D_MODEL), W_FOX ** -0.5),
        'w_branch_band': nrm(ks[17], (DEPTH, W_BAND, D_MODEL), W_BAND ** -0.5),
        'w_out': nrm(ks[18], (DEPTH, D_MODEL, D_MODEL), D_MODEL ** -0.5),
        'w_query': nrm(ks[19], (DEPTH, D_MODEL, PEER_HEADS * PEER_QDIM), D_MODEL ** -0.5),
        'sub_keys': nrm(ks[20], (DEPTH, PEER_HEADS, 2, N_KEYS, PEER_HALF), PEER_HALF ** -0.5),
        'expert_u': nrm(ks[21], (DEPTH, N_EXPERTS, D_MODEL), D_MODEL ** -0.5),
        'expert_v': nrm(ks[22], (DEPTH, N_EXPERTS, D_MODEL), 1.0),
        'norm_final': 1.0 + nrm(ks[23], (D_MODEL,), 0.01),
    }


def reference(x_prompt, x_sample, c_prompt, c_sample, cache_fox_k, cache_fox_v, cache_fox_logf,
              cache_band_k, cache_band_v, w_ada, b_ada, norm_mix, norm_ffn, w_in, b_forget, rel_bias,
              w_branch_fox, w_branch_band, w_out, w_query, sub_keys, expert_u, expert_v, norm_final):
    xp, xs = x_prompt, x_sample
    st_p, st_s = [], []
    for l in range(DEPTH):
        lw = (w_ada[l], b_ada[l], norm_mix[l], norm_ffn[l], w_in[l], b_forget[l], w_branch_fox[l],
              w_branch_band[l], w_out[l], w_query[l], sub_keys[l], expert_u[l], expert_v[l])
        band_p = functools.partial(_band_prompt, rel_bias=rel_bias[l])
        xp, sp = _block(xp, c_prompt, _fox_prompt, band_p, *lw)
        fox_s = functools.partial(_fox_sample, cache_k=cache_fox_k[l], cache_v=cache_fox_v[l],
                                  cache_logf=cache_fox_logf[l])
        band_s = functools.partial(_band_sample, cache_k=cache_band_k[l], cache_v=cache_band_v[l],
                                   rel_bias=rel_bias[l])
        xs, ss = _block(xs, c_sample, fox_s, band_s, *lw)
        st_p.append(sp)
        st_s.append(ss)
    fox_k_p, fox_v_p, fox_logf_p, band_k_p, band_v_p = [jnp.stack([s[i] for s in st_p]) for i in range(5)]
    fox_k_s, fox_v_s, fox_logf_s, band_k_s, band_v_s = [jnp.stack([s[i] for s in st_s]) for i in range(5)]
    y_prompt = _rmsnorm(xp, norm_final)
    y_sample = _rmsnorm(xs, norm_final)
    return (y_prompt, y_sample, fox_k_p, fox_v_p, fox_logf_p, band_k_p, band_v_p,
            fox_k_s, fox_v_s, fox_logf_s, band_k_s, band_v_s)
```

```python
import functools

import numpy as np
import jax
import jax.numpy as jnp
from jax import lax
from jax.experimental import pallas as pl
from jax.experimental.pallas import tpu as pltpu

D_MODEL = 2048
HEAD_DIM = 128
N_HEADS = 8
W_ATT = N_HEADS * HEAD_DIM
CHUNK = 64
LEFT_CHUNKS = 8
BAND_WINDOW = LEFT_CHUNKS * CHUNK
REL_MAX = 128
N_KEYS = 128
PEER_HEADS = 8
PEER_TOPK = 16
PEER_HALF = 128
PEER_SLOTS = PEER_HEADS * PEER_TOPK
EPS = 1e-6
ATT_SCALE = HEAD_DIM ** -0.5

LANES = 128
SUBLANES = 8
VMEM_LIMIT = 56 * 1024 * 1024
ROW_TILES = D_MODEL // LANES

F32 = jnp.float32
BF16 = jnp.bfloat16
NT_DIMS = (((1,), (1,)), ((), ()))


def _params(*sem):
    return pltpu.CompilerParams(dimension_semantics=sem, vmem_limit_bytes=VMEM_LIMIT)


def _nt(a, b):
    return lax.dot_general(a, b, NT_DIMS, preferred_element_type=F32)


def _norm_mod(x, gain, sc, sh):
    y = x * lax.rsqrt(jnp.mean(x * x, axis=-1, keepdims=True) + EPS)
    return (y * gain) * (1.0 + sc) + sh


def _split_bf16(x):
    hi = x.astype(BF16)
    lo = (x - hi.astype(F32)).astype(BF16)
    return hi, lo


def _adaln_kernel(c_ref, w_ref, b_ref, o_ref):
    c = c_ref[...]
    a = (c * (1.0 / (1.0 + jnp.exp(-c)))).astype(BF16)
    o_ref[...] = jnp.dot(a, w_ref[...].astype(BF16), preferred_element_type=F32) + b_ref[...]


def _adaln(c, w_ada, b_ada):
    rows, tn = c.shape[0], 1024
    n = w_ada.shape[1]
    return pl.pallas_call(
        _adaln_kernel,
        grid=(n // tn,),
        in_specs=[pl.BlockSpec((rows, D_MODEL), lambda j: (0, 0)),
                  pl.BlockSpec((D_MODEL, tn), lambda j: (0, j)),
                  pl.BlockSpec((1, tn), lambda j: (0, j))],
        out_specs=pl.BlockSpec((rows, tn), lambda j: (0, j)),
        out_shape=jax.ShapeDtypeStruct((rows, n), F32),
        compiler_params=_params("arbitrary"),
        name="adaln",
    )(c, w_ada, b_ada.reshape(1, n))


def _inproj_kernel(x_ref, gain_ref, sc_ref, sh_ref, w_ref, wff_ref, bf_ref, h_ref, logf_ref, xm_scr):
    @pl.when(pl.program_id(1) == 0)
    def _():
        xm = _norm_mod(x_ref[...], gain_ref[...], sc_ref[...], sh_ref[...])
        hi, lo = _split_bf16(xm)
        xm_scr[...] = hi
        whi, wlo = _split_bf16(wff_ref[...])
        z = _nt(whi, hi) + _nt(whi, lo) + _nt(wlo, hi) + bf_ref[...]
        logf_ref[...] = jnp.minimum(z, 0.0) - jnp.log1p(jnp.exp(-jnp.abs(z)))

    h_ref[...] = jnp.dot(xm_scr[...], w_ref[...], preferred_element_type=F32)


def _inproj(x2d, gain, sc3, sh3, mod_spec, w_cat, wff_t, bf_col, tm):
    rows = x2d.shape[0]
    tn = W_ATT
    nparts = w_cat.shape[1] // tn
    return pl.pallas_call(
        _inproj_kernel,
        grid=(rows // tm, nparts),
        in_specs=[pl.BlockSpec((tm, D_MODEL), lambda i, j: (i, 0)),
                  pl.BlockSpec((1, D_MODEL), lambda i, j: (0, 0)),
                  mod_spec, mod_spec,
                  pl.BlockSpec((D_MODEL, tn), lambda i, j: (0, j)),
                  pl.BlockSpec((N_HEADS, D_MODEL), lambda i, j: (0, 0)),
                  pl.BlockSpec((N_HEADS, 1), lambda i, j: (0, 0))],
        out_specs=[pl.BlockSpec((None, tm, tn), lambda i, j: (j, i, 0)),
                   pl.BlockSpec((N_HEADS, tm), lambda i, j: (0, i))],
        out_shape=[jax.ShapeDtypeStruct((nparts, rows, tn), F32),
                   jax.ShapeDtypeStruct((N_HEADS, rows), F32)],
        scratch_shapes=[pltpu.VMEM((tm, D_MODEL), BF16)],
        compiler_params=_params("arbitrary", "arbitrary"),
        name="inproj",
    )(x2d, gain, sc3, sh3, w_cat, wff_t, bf_col)


CUMSUM_CHUNK = 512


def _cumsum_kernel(x_ref, o_ref):
    c = CUMSUM_CHUNK
    r = lax.broadcasted_iota(jnp.int32, (c, c), 0)
    q = lax.broadcasted_iota(jnp.int32, (c, c), 1)
    tri = jnp.where(r <= q, 1.0, 0.0).astype(BF16)
    carry = jnp.zeros((SUBLANES, 1), F32)
    for k in range(x_ref.shape[1] // c):
        x = x_ref[:, k * c:(k + 1) * c]
        hi = x.astype(BF16)
        r1 = x - hi.astype(F32)
        mid = r1.astype(BF16)
        lo = (r1 - mid.astype(F32)).astype(BF16)
        y = (jnp.dot(hi, tri, preferred_element_type=F32) + jnp.dot(mid, tri, preferred_element_type=F32)
             + jnp.dot(lo, tri, preferred_element_type=F32)) + carry
        o_ref[:, k * c:(k + 1) * c] = y
        carry = y[:, c - 1:c]


def _cumsum_rows(x):
    rows, n = x.shape
    return pl.pallas_call(
        _cumsum_kernel,
        grid=(rows // SUBLANES,),
        in_specs=[pl.BlockSpec((SUBLANES, n), lambda i: (i, 0))],
        out_specs=pl.BlockSpec((SUBLANES, n), lambda i: (i, 0)),
        out_shape=jax.ShapeDtypeStruct((rows, n), F32),
        compiler_params=_params("arbitrary"),
        name="cumsum",
    )(x)


def _softmax_update(s, v, m_prev, l_prev, acc_prev):
    m_new = jnp.maximum(m_prev, jnp.max(s, axis=1, keepdims=True))
    alpha = jnp.exp(m_prev - m_new)
    p = jnp.exp(s - m_new)
    l_new = alpha * l_prev + jnp.sum(p, axis=1, keepdims=True)
    acc_new = alpha * acc_prev + jnp.dot(p.astype(BF16), v, preferred_element_type=F32)
    return m_new, l_new, acc_new


def _fox_prompt_kernel(q_ref, k_ref, v_ref, f_ref, o_ref, m_scr, l_scr, acc_scr, *, tq):
    qi, ki = pl.program_id(2), pl.program_id(3)

    @pl.when(ki == 0)
    def _():
        m_scr[...] = jnp.full(m_scr.shape, -jnp.inf, F32)
        l_scr[...] = jnp.zeros(l_scr.shape, F32)
        acc_scr[...] = jnp.zeros(acc_scr.shape, F32)

    @pl.when(ki <= qi)
    def _():
        s = _nt(q_ref[...].astype(BF16), k_ref[...].astype(BF16)) * ATT_SCALE - f_ref[...]
        row = qi * tq + lax.broadcasted_iota(jnp.int32, s.shape, 0)
        col = ki * tq + lax.broadcasted_iota(jnp.int32, s.shape, 1)
        s = jnp.where(col <= row, s, -jnp.inf)
        m, l, acc = _softmax_update(s, v_ref[...].astype(BF16), m_scr[...], l_scr[...], acc_scr[...])
        m_scr[...], l_scr[...], acc_scr[...] = m, l, acc

    @pl.when(ki == qi)
    def _():
        o_ref[...] = (acc_scr[...] / l_scr[...]).astype(o_ref.dtype)


def _fox_prompt(h, f3, batch, seq, tq=512):
    nq = seq // tq
    rows = batch * seq
    kv = lambda part: pl.BlockSpec(
        (None, tq, HEAD_DIM), lambda b, hh, qi, ki: (part, b * nq + jnp.minimum(ki, qi), hh))
    return pl.pallas_call(
        functools.partial(_fox_prompt_kernel, tq=tq),
        grid=(batch, N_HEADS, nq, nq),
        in_specs=[pl.BlockSpec((None, tq, HEAD_DIM), lambda b, hh, qi, ki: (0, b * nq + qi, hh)),
                  kv(1), kv(2),
                  pl.BlockSpec((None, 1, tq), lambda b, hh, qi, ki: (hh * batch + b, 0, jnp.minimum(ki, qi)))],
        out_specs=pl.BlockSpec((tq, HEAD_DIM), lambda b, hh, qi, ki: (b * nq + qi, hh)),
        out_shape=jax.ShapeDtypeStruct((rows, W_ATT), BF16),
        scratch_shapes=[pltpu.VMEM((tq, 1), F32), pltpu.VMEM((tq, 1), F32), pltpu.VMEM((tq, HEAD_DIM), F32)],
        compiler_params=_params("arbitrary", "arbitrary", "arbitrary", "arbitrary"),
        name="fox_prompt",
    )(h, h, h, f3)


BAND_TQ = BAND_WINDOW


def _band_prompt_kernel(q_ref, ko_ref, kl_ref, vo_ref, vl_ref, bo_ref, bl_ref, o_ref):
    q = q_ref[...].astype(BF16)
    so = _nt(q, ko_ref[...].astype(BF16)) * ATT_SCALE + bo_ref[...]
    sl = _nt(q, kl_ref[...].astype(BF16)) * ATT_SCALE + bl_ref[...]
    sl = jnp.where(pl.program_id(2) > 0, sl, -jnp.inf)
    m = jnp.maximum(jnp.max(so, axis=1, keepdims=True), jnp.max(sl, axis=1, keepdims=True))
    po, pp = jnp.exp(so - m), jnp.exp(sl - m)
    l = jnp.sum(po, axis=1, keepdims=True) + jnp.sum(pp, axis=1, keepdims=True)
    acc = (jnp.dot(po.astype(BF16), vo_ref[...].astype(BF16), preferred_element_type=F32)
           + jnp.dot(pp.astype(BF16), vl_ref[...].astype(BF16), preferred_element_type=F32))
    o_ref[...] = (acc / l).astype(o_ref.dtype)


def _band_prompt(h, bias_own, bias_left, batch, seq):
    tq = BAND_TQ
    nq = seq // tq
    own = lambda part: pl.BlockSpec((None, tq, HEAD_DIM), lambda b, hh, qi: (part, b * nq + qi, hh))
    left = lambda part: pl.BlockSpec(
        (None, tq, HEAD_DIM), lambda b, hh, qi: (part, b * nq + jnp.maximum(qi - 1, 0), hh))
    bias = pl.BlockSpec((None, tq, tq), lambda b, hh, qi: (hh, 0, 0))
    return pl.pallas_call(
        _band_prompt_kernel,
        grid=(batch, N_HEADS, nq),
        in_specs=[own(3), own(4), left(4), own(5), left(5), bias, bias],
        out_specs=pl.BlockSpec((tq, HEAD_DIM), lambda b, hh, qi: (b * nq + qi, hh)),
        out_shape=jax.ShapeDtypeStruct((batch * seq, W_ATT), BF16),
        compiler_params=_params("arbitrary", "arbitrary", "arbitrary"),
        name="band_prompt",
    )(h, h, h, h, h, bias_own, bias_left)


def _band_prompt_bias(rel_bias):
    r = np.arange(BAND_TQ)[:, None]
    c = np.arange(BAND_TQ)[None, :]
    idx_own = np.clip(r - c, -REL_MAX, REL_MAX) + REL_MAX
    idx_left = np.clip(BAND_TQ + r - c, -REL_MAX, REL_MAX) + REL_MAX
    own = jnp.where((c // CHUNK <= r // CHUNK)[None], rel_bias[:, idx_own], -jnp.inf)
    left = jnp.where((c // CHUNK >= r // CHUNK)[None], rel_bias[:, idx_left], -jnp.inf)
    return own.astype(F32), left.astype(F32)


def _fox_sample_kernel(q_ref, kc_ref, vc_ref, fc_ref, kn_ref, vn_ref, fn_ref, o_ref, m_scr, l_scr, acc_scr,
                       *, nk, t_new):
    ki = pl.program_id(1)

    @pl.when(ki == 0)
    def _():
        m_scr[...] = jnp.full(m_scr.shape, -jnp.inf, F32)
        l_scr[...] = jnp.zeros(l_scr.shape, F32)
        acc_scr[...] = jnp.zeros(acc_scr.shape, F32)

    for hh in range(N_HEADS):
        sl = slice(hh * HEAD_DIM, (hh + 1) * HEAD_DIM)
        q = q_ref[:, sl].astype(BF16)
        s = _nt(q, kc_ref[:, sl].astype(BF16)) * ATT_SCALE - fc_ref[hh:hh + 1, :]
        m, l, acc = _softmax_update(s, vc_ref[:, sl].astype(BF16), m_scr[hh], l_scr[hh], acc_scr[hh])
        m_scr[hh], l_scr[hh], acc_scr[hh] = m, l, acc

    @pl.when(ki == nk - 1)
    def _():
        for hh in range(N_HEADS):
            sl = slice(hh * HEAD_DIM, (hh + 1) * HEAD_DIM)
            q = q_ref[:, sl].astype(BF16)
            s = _nt(q, kn_ref[:, sl].astype(BF16)) * ATT_SCALE - fn_ref[hh:hh + 1, 0:t_new]
            row = lax.broadcasted_iota(jnp.int32, s.shape, 0)
            col = lax.broadcasted_iota(jnp.int32, s.shape, 1)
            s = jnp.where(col <= row, s, -jnp.inf)
            m, l, acc = _softmax_update(s, vn_ref[:, sl].astype(BF16), m_scr[hh], l_scr[hh], acc_scr[hh])
            o_ref[:, sl] = (acc / l).astype(o_ref.dtype)


def _fox_sample(h, cache_k, cache_v, f_cache, f_new, batch, t_new, tk=1024):
    past = cache_k.shape[1]
    nk = past // tk
    new = lambda part: pl.BlockSpec((None, t_new, W_ATT), lambda b, ki: (part, b, 0))
    cache = pl.BlockSpec((None, tk, W_ATT), lambda b, ki: (b, ki, 0))
    return pl.pallas_call(
        functools.partial(_fox_sample_kernel, nk=nk, t_new=t_new),
        grid=(batch, nk),
        in_specs=[new(0), cache, cache,
                  pl.BlockSpec((None, N_HEADS, tk), lambda b, ki: (b, 0, ki)),
                  new(1), new(2),
                  pl.BlockSpec((None, N_HEADS, LANES), lambda b, ki: (b, 0, 0))],
        out_specs=pl.BlockSpec((t_new, W_ATT), lambda b, ki: (b, 0)),
        out_shape=jax.ShapeDtypeStruct((batch * t_new, W_ATT), BF16),
        scratch_shapes=[pltpu.VMEM((N_HEADS, t_new, 1), F32), pltpu.VMEM((N_HEADS, t_new, 1), F32),
                        pltpu.VMEM((N_HEADS, t_new, HEAD_DIM), F32)],
        compiler_params=_params("arbitrary", "arbitrary"),
        name="fox_sample",
    )(h, cache_k, cache_v, f_cache, h, h, f_new)


def _band_sample_kernel(q_ref, kc_ref, vc_ref, kn_ref, vn_ref, bc_ref, bn_ref, o_ref):
    for hh in range(N_HEADS):
        sl = slice(hh * HEAD_DIM, (hh + 1) * HEAD_DIM)
        q = q_ref[:, sl].astype(BF16)
        sc = _nt(q, kc_ref[:, sl].astype(BF16)) * ATT_SCALE + bc_ref[hh]
        sn = _nt(q, kn_ref[:, sl].astype(BF16)) * ATT_SCALE + bn_ref[hh]
        m = jnp.maximum(jnp.max(sc, axis=1, keepdims=True), jnp.max(sn, axis=1, keepdims=True))
        pc, pn = jnp.exp(sc - m), jnp.exp(sn - m)
        l = jnp.sum(pc, axis=1, keepdims=True) + jnp.sum(pn, axis=1, keepdims=True)
        acc = (jnp.dot(pc.astype(BF16), vc_ref[:, sl].astype(BF16), preferred_element_type=F32)
               + jnp.dot(pn.astype(BF16), vn_ref[:, sl].astype(BF16), preferred_element_type=F32))
        o_ref[:, sl] = (acc / l).astype(o_ref.dtype)


def _band_sample(h, cache_k, cache_v, bias_c, bias_n, batch, t_new):
    win = cache_k.shape[1]
    new = lambda part: pl.BlockSpec((None, t_new, W_ATT), lambda b: (part, b, 0))
    cache = pl.BlockSpec((None, win, W_ATT), lambda b: (b, 0, 0))
    return pl.pallas_call(
        _band_sample_kernel,
        grid=(batch,),
        in_specs=[new(3), cache, cache, new(4), new(5),
                  pl.BlockSpec((N_HEADS, t_new, win), lambda b: (0, 0, 0)),
                  pl.BlockSpec((N_HEADS, t_new, t_new), lambda b: (0, 0, 0))],
        out_specs=pl.BlockSpec((t_new, W_ATT), lambda b: (b, 0)),
        out_shape=jax.ShapeDtypeStruct((batch * t_new, W_ATT), BF16),
        compiler_params=_params("arbitrary"),
        name="band_sample",
    )(h, cache_k, cache_v, h, h, bias_c, bias_n)


def _band_sample_bias(rel_bias, win, t_new):
    t = np.arange(t_new)[:, None]
    d = win + t - np.arange(win + t_new)[None, :]
    b = rel_bias[:, np.clip(d, -REL_MAX, REL_MAX) + REL_MAX].astype(F32)
    return b[:, :, :win], b[:, :, win:]


def _merge_kernel(ya_ref, yb_ref, ga0_ref, ga1_ref, gb0_ref, gb1_ref, x_ref, g1_ref,
                  wf_ref, wb_ref, wo_ref, o_ref):
    a = jnp.dot(ya_ref[...], wf_ref[...], preferred_element_type=F32)
    b = jnp.dot(yb_ref[...], wb_ref[...], preferred_element_type=F32)
    sig = lambda z: 1.0 / (1.0 + jnp.exp(-z))
    half = D_MODEL // 2
    m0 = (sig(ga0_ref[...]) * a[:, :half] + sig(gb0_ref[...]) * b[:, :half]).astype(BF16)
    m1 = (sig(ga1_ref[...]) * a[:, half:] + sig(gb1_ref[...]) * b[:, half:]).astype(BF16)
    y = (jnp.dot(m0, wo_ref[:half, :], preferred_element_type=F32)
         + jnp.dot(m1, wo_ref[half:, :], preferred_element_type=F32))
    o_ref[...] = x_ref[...] + g1_ref[...] * y


def _merge(ya, yb, h, x2d, g13, mod_spec, wf, wb, wo, tm):
    rows = x2d.shape[0]
    part = lambda p: pl.BlockSpec((None, tm, W_ATT), lambda i: (p, i, 0))
    const = lambda shape: pl.BlockSpec(shape, lambda i: (0, 0), pipeline_mode=pl.Buffered(1))
    return pl.pallas_call(
        _merge_kernel,
        grid=(rows // tm,),
        in_specs=[pl.BlockSpec((tm, W_ATT), lambda i: (i, 0)), pl.BlockSpec((tm, W_ATT), lambda i: (i, 0)),
                  part(6), part(7), part(8), part(9),
                  pl.BlockSpec((tm, D_MODEL), lambda i: (i, 0)),
                  mod_spec,
                  const((W_ATT, D_MODEL)), const((W_ATT, D_MODEL)), const((D_MODEL, D_MODEL))],
        out_specs=pl.BlockSpec((tm, D_MODEL), lambda i: (i, 0)),
        out_shape=jax.ShapeDtypeStruct((rows, D_MODEL), F32),
        compiler_params=_params("arbitrary"),
        name="merge",
    )(ya, yb, h, h, h, h, x2d, g13, wf, wb, wo)


def _peer_query_kernel(x_ref, gain_ref, sc_ref, sh_ref, w_ref, q_ref, xf_ref, xm_scr):
    @pl.when(pl.program_id(1) == 0)
    def _():
        xf = _norm_mod(x_ref[...], gain_ref[...], sc_ref[...], sh_ref[...])
        xf_ref[...] = xf
        xm_scr[...] = xf.astype(BF16)

    q_ref[...] = jnp.dot(xm_scr[...], w_ref[...], preferred_element_type=F32)


def _peer_query(x2d, gain, sc3, sh3, mod_spec, wq, tm):
    rows = x2d.shape[0]
    tn = 1024
    return pl.pallas_call(
        _peer_query_kernel,
        grid=(rows // tm, wq.shape[1] // tn),
        in_specs=[pl.BlockSpec((tm, D_MODEL), lambda i, j: (i, 0)),
                  pl.BlockSpec((1, D_MODEL), lambda i, j: (0, 0)),
                  mod_spec, mod_spec,
                  pl.BlockSpec((D_MODEL, tn), lambda i, j: (0, j))],
        out_specs=[pl.BlockSpec((tm, tn), lambda i, j: (i, j)),
                   pl.BlockSpec((tm, D_MODEL), lambda i, j: (i, 0))],
        out_shape=[jax.ShapeDtypeStruct((rows, wq.shape[1]), F32),
                   jax.ShapeDtypeStruct((rows, D_MODEL), F32)],
        scratch_shapes=[pltpu.VMEM((tm, D_MODEL), BF16)],
        compiler_params=_params("arbitrary", "arbitrary"),
        name="peer_query",
    )(x2d, gain, sc3, sh3, wq)


def _top_k_rows(s, k):
    n = s.shape[0]
    iota = lax.broadcasted_iota(jnp.int32, s.shape, 0)
    vals, idxs = [], []
    for _ in range(k):
        m = jnp.max(s, axis=0, keepdims=True)
        i = jnp.min(jnp.where(s == m, iota, n), axis=0, keepdims=True)
        vals.append(m)
        idxs.append(i)
        s = jnp.where(iota == i, -jnp.inf, s)
    return jnp.concatenate(vals, axis=0), jnp.concatenate(idxs, axis=0)


def _select_rows(table, sel):
    out = jnp.zeros(sel.shape, table.dtype)
    for r in range(table.shape[0]):
        out = jnp.where(sel == r, table[r:r + 1, :], out)
    return out


def _route_kernel(q_ref, sk_ref, e_ref, g_ref):
    k = PEER_TOPK
    e_all, g_all = [], []
    for hh in range(PEER_HEADS):
        halves = []
        for c in range(2):
            col = (2 * hh + c) * PEER_HALF
            qh = q_ref[:, col:col + PEER_HALF].astype(BF16)
            halves.append(_top_k_rows(_nt(sk_ref[hh, c].astype(BF16), qh), k))
        (s1, i1), (s2, i2) = halves
        cand = jnp.concatenate([s1[a:a + 1, :] + s2 for a in range(k)], axis=0)
        top_s, top_c = _top_k_rows(cand, k)
        e_all.append(_select_rows(i1, top_c >> (k.bit_length() - 1)) * N_KEYS + _select_rows(i2, top_c & (k - 1)))
        p = jnp.exp(top_s - top_s[0:1, :])
        g_all.append(p / jnp.sum(p, axis=0, keepdims=True))
    e_ref[...] = jnp.concatenate(e_all, axis=0).T
    g_ref[...] = jnp.concatenate(g_all, axis=0).T


def _route(qp, sub_keys, tt):
    rows = qp.shape[0]
    return pl.pallas_call(
        _route_kernel,
        grid=(rows // tt,),
        in_specs=[pl.BlockSpec((tt, qp.shape[1]), lambda i: (i, 0)),
                  pl.BlockSpec(sub_keys.shape, lambda i: (0, 0, 0, 0))],
        out_specs=[pl.BlockSpec((tt, PEER_SLOTS), lambda i: (i, 0)),
                   pl.BlockSpec((tt, PEER_SLOTS), lambda i: (i, 0))],
        out_shape=[jax.ShapeDtypeStruct((rows, PEER_SLOTS), jnp.int32),
                   jax.ShapeDtypeStruct((rows, PEER_SLOTS), F32)],
        compiler_params=_params("arbitrary"),
        name="peer_route",
    )(qp, sub_keys)


PEER_NBUF = 4
UV_ROWS = 2 * ROW_TILES


def _peer_kernel(idx_ref, g_ref, xf_ref, x1_ref, g2_ref, nf_ref, uv_ref, y_ref,
                 buf, p_scr, wb_scr, sem, *, tb):
    nbuf = PEER_NBUF
    group = SUBLANES

    def issue(t, slot):
        def body(jo, carry):
            for ji in range(group):
                j = jo * group + ji
                e = idx_ref[0, t * PEER_SLOTS + j]
                pltpu.make_async_copy(uv_ref.at[e], buf.at[slot, j], sem.at[slot]).start()
            return carry
        lax.fori_loop(0, PEER_SLOTS // group, body, 0)

    def wait(slot):
        pltpu.make_async_copy(uv_ref.at[pl.ds(0, PEER_SLOTS)], buf.at[slot], sem.at[slot]).wait()

    for s in range(nbuf - 1):
        issue(s, s)

    eye = (lax.broadcasted_iota(jnp.int32, (PEER_SLOTS, PEER_SLOTS), 0)
           == lax.broadcasted_iota(jnp.int32, (PEER_SLOTS, PEER_SLOTS), 1))

    def token(t, carry):
        slot = t % nbuf

        @pl.when(t + nbuf - 1 < tb)
        def _():
            issue(t + nbuf - 1, (t + nbuf - 1) % nbuf)

        wait(slot)
        x_lo = xf_ref[t, 0:SUBLANES, :]
        x_hi = xf_ref[t, SUBLANES:ROW_TILES, :]

        def dot_body(jo, c):
            for ji in range(group):
                j = jo * group + ji
                p = buf[slot, j, 0:SUBLANES, :] * x_lo + buf[slot, j, SUBLANES:ROW_TILES, :] * x_hi
                p_scr[pl.ds(pl.multiple_of(j * SUBLANES, SUBLANES), SUBLANES), :] = p
            return c
        lax.fori_loop(0, PEER_SLOTS // group, dot_body, 0)

        ps = p_scr[pl.ds(0, PEER_SLOTS, stride=SUBLANES), :]
        for s in range(1, SUBLANES):
            ps = ps + p_scr[pl.ds(s, PEER_SLOTS, stride=SUBLANES), :]
        hcol = jnp.sum(ps, axis=1, keepdims=True)
        gcol = jnp.sum(jnp.where(eye, g_ref[pl.ds(t, 1), :], 0.0), axis=1, keepdims=True)
        w = gcol * (0.5 * hcol * (1.0 + lax.erf(hcol * (2.0 ** -0.5))))
        wb_scr[...] = jnp.broadcast_to(w, (PEER_SLOTS, LANES))

        def acc_body(jo, accs):
            accs = list(accs)
            for ji in range(group):
                j = jo * group + ji
                wv = wb_scr[pl.ds(j, 1), :]
                a = 2 * (ji % 4)
                accs[a] = accs[a] + buf[slot, j, ROW_TILES:ROW_TILES + SUBLANES, :] * wv
                accs[a + 1] = accs[a + 1] + buf[slot, j, ROW_TILES + SUBLANES:UV_ROWS, :] * wv
            return tuple(accs)
        zero = jnp.zeros((SUBLANES, LANES), F32)
        accs = lax.fori_loop(0, PEER_SLOTS // group, acc_body, (zero,) * 8)
        o_lo = (accs[0] + accs[2]) + (accs[4] + accs[6])
        o_hi = (accs[1] + accs[3]) + (accs[5] + accs[7])

        z_lo = x1_ref[t, 0:SUBLANES, :] + g2_ref[0:SUBLANES, :] * o_lo
        z_hi = x1_ref[t, SUBLANES:ROW_TILES, :] + g2_ref[SUBLANES:ROW_TILES, :] * o_hi
        ms = (jnp.sum(z_lo * z_lo, keepdims=True) + jnp.sum(z_hi * z_hi, keepdims=True)) * (1.0 / D_MODEL)
        inv = lax.rsqrt(ms + EPS)
        y_ref[t, 0:SUBLANES, :] = z_lo * inv * nf_ref[0:SUBLANES, :]
        y_ref[t, SUBLANES:ROW_TILES, :] = z_hi * inv * nf_ref[SUBLANES:ROW_TILES, :]
        return carry

    lax.fori_loop(0, tb, token, 0)


def _peer(idx, gate, xf, x1, g2t, g2_spec, nf_t, uv, tb):
    rows = xf.shape[0]
    nblk = rows // tb
    tok = lambda: pl.BlockSpec((tb, ROW_TILES, LANES), lambda i: (i, 0, 0))
    return pl.pallas_call(
        functools.partial(_peer_kernel, tb=tb),
        grid=(nblk,),
        in_specs=[pl.BlockSpec((None, 1, tb * PEER_SLOTS), lambda i: (i, 0, 0), memory_space=pltpu.SMEM),
                  pl.BlockSpec((tb, PEER_SLOTS), lambda i: (i, 0)),
                  tok(), tok(), g2_spec,
                  pl.BlockSpec((ROW_TILES, LANES), lambda i: (0, 0)),
                  pl.BlockSpec(memory_space=pl.ANY)],
        out_specs=tok(),
        out_shape=jax.ShapeDtypeStruct((rows, ROW_TILES, LANES), F32),
        scratch_shapes=[pltpu.VMEM((PEER_NBUF, PEER_SLOTS, UV_ROWS, LANES), F32),
                        pltpu.VMEM((PEER_SLOTS * SUBLANES, LANES), F32),
                        pltpu.VMEM((PEER_SLOTS, LANES), F32),
                        pltpu.SemaphoreType.DMA((PEER_NBUF,))],
        compiler_params=_params("arbitrary"),
        name="peer_experts",
    )(idx.reshape(nblk, 1, tb * PEER_SLOTS), gate, xf.reshape(rows, ROW_TILES, LANES),
      x1.reshape(rows, ROW_TILES, LANES), g2t, nf_t, uv)


def _stream(x, mods, weights, attend, tm, tb):
    batch, seq, _ = x.shape
    rows = batch * seq
    tm = min(tm, rows)
    sh1, sc1, g1, sh2, sc2, g2 = mods
    x2d = x.reshape(rows, D_MODEL)
    if seq >= tm:
        per = seq // tm
        mod3 = lambda m: m.reshape(batch, 1, D_MODEL)
        mod_spec2 = pl.BlockSpec((None, 1, D_MODEL), lambda i, j: (i // per, 0, 0))
        mod_spec1 = pl.BlockSpec((None, 1, D_MODEL), lambda i: (i // per, 0, 0))
    else:
        mod3 = lambda m: jnp.repeat(m, seq, axis=0).reshape(rows // tm, tm, D_MODEL)
        mod_spec2 = pl.BlockSpec((None, tm, D_MODEL), lambda i, j: (i, 0, 0))
        mod_spec1 = pl.BlockSpec((None, tm, D_MODEL), lambda i: (i, 0, 0))

    h, logf_t = _inproj(x2d, weights["norm_mix"], mod3(sc1), mod3(sh1), mod_spec2,
                        weights["w_cat"], weights["wff_t"], weights["bf_col"], tm)
    ya, yb = attend(h, logf_t)
    tm2 = min(tm, 256)
    if seq >= tm2:
        per2 = seq // tm2
        g13 = g1.reshape(batch, 1, D_MODEL)
        g1_spec = pl.BlockSpec((None, 1, D_MODEL), lambda i: (i // per2, 0, 0))
    else:
        g13 = jnp.repeat(g1, seq, axis=0).reshape(rows // tm2, tm2, D_MODEL)
        g1_spec = pl.BlockSpec((None, tm2, D_MODEL), lambda i: (i, 0, 0))
    x1 = _merge(ya, yb, h, x2d, g13, g1_spec, weights["wf"], weights["wb"], weights["wo"], tm2)
    qp, xf = _peer_query(x1, weights["norm_ffn"], mod3(sc2), mod3(sh2), mod_spec2, weights["wq"], tm)
    idx, gate = _route(qp, weights["sub_keys"], 256)
    per_tb = seq // tb
    g2t = g2.reshape(batch, ROW_TILES, LANES)
    g2_spec = pl.BlockSpec((None, ROW_TILES, LANES), lambda i: (i // per_tb, 0, 0))
    y = _peer(idx, gate, xf, x1, g2t, g2_spec, weights["nf_t"], weights["uv"], tb)
    return y.reshape(batch, seq, D_MODEL), h, logf_t


def kernel(x_prompt, x_sample, c_prompt, c_sample, cache_fox_k, cache_fox_v, cache_fox_logf, cache_band_k, cache_band_v, w_ada, b_ada, norm_mix, norm_ffn, w_in, b_forget, rel_bias, w_branch_fox, w_branch_band, w_out, w_query, sub_keys, expert_u, expert_v, norm_final):
    bp, sp, _ = x_prompt.shape
    bs, ts, _ = x_sample.shape
    past = cache_fox_k.shape[2]
    win = cache_band_k.shape[2]
    n_exp = expert_u.shape[1]

    w = w_in[0]
    o_ff = 3 * W_ATT
    weights = {
        "w_cat": jnp.concatenate([w[:, :o_ff], w[:, o_ff + N_HEADS:]], axis=1).astype(BF16),
        "wff_t": w[:, o_ff:o_ff + N_HEADS].T,
        "bf_col": b_forget[0].reshape(N_HEADS, 1),
        "norm_mix": norm_mix[0].reshape(1, D_MODEL),
        "norm_ffn": norm_ffn[0].reshape(1, D_MODEL),
        "wf": w_branch_fox[0].astype(BF16),
        "wb": w_branch_band[0].astype(BF16),
        "wo": w_out[0].astype(BF16),
        "wq": w_query[0].astype(BF16),
        "sub_keys": sub_keys[0],
        "nf_t": norm_final.reshape(ROW_TILES, LANES),
        "uv": jnp.concatenate([expert_u[0].reshape(n_exp, ROW_TILES, LANES),
                               expert_v[0].reshape(n_exp, ROW_TILES, LANES)], axis=1),
    }
    rb = rel_bias[0]
    bias_own, bias_left = _band_prompt_bias(rb)
    bias_c, bias_n = _band_sample_bias(rb, win, ts)

    ada = _adaln(jnp.concatenate([c_prompt, c_sample], axis=0), w_ada[0], b_ada[0])
    mods_p = [ada[:bp, i * D_MODEL:(i + 1) * D_MODEL] for i in range(6)]
    mods_s = [ada[bp:, i * D_MODEL:(i + 1) * D_MODEL] for i in range(6)]

    def attend_prompt(h, logf_t):
        f = _cumsum_rows(logf_t.reshape(N_HEADS * bp, sp))
        ya = _fox_prompt(h, f.reshape(N_HEADS * bp, 1, sp), bp, sp)
        yb = _band_prompt(h, bias_own, bias_left, bp, sp)
        return ya, yb

    y_prompt, h_p, logf_p = _stream(x_prompt, mods_p, weights, attend_prompt, tm=512, tb=32)

    kc = cache_fox_k[0].reshape(bs, past, W_ATT)
    vc = cache_fox_v[0].reshape(bs, past, W_ATT)
    bkc = cache_band_k[0].reshape(bs, win, W_ATT)
    bvc = cache_band_v[0].reshape(bs, win, W_ATT)

    def attend_sample(h, logf_t):
        lf_new = logf_t.reshape(N_HEADS, bs, ts).transpose(1, 0, 2)
        lf_all = jnp.concatenate([cache_fox_logf[0].transpose(0, 2, 1), lf_new], axis=2)
        n_pad = -(past + ts) % CUMSUM_CHUNK
        lf_all = jnp.pad(lf_all, ((0, 0), (0, 0), (0, n_pad)))
        f = _cumsum_rows(lf_all.reshape(bs * N_HEADS, -1)).reshape(bs, N_HEADS, -1)
        ya = _fox_sample(h, kc, vc, f[:, :, :past], f[:, :, past:past + LANES], bs, ts)
        yb = _band_sample(h, bkc, bvc, bias_c, bias_n, bs, ts)
        return ya, yb

    y_sample, h_s, logf_s = _stream(x_sample, mods_s, weights, attend_sample, tm=512, tb=ts)

    heads = lambda part, b, t: part.reshape(1, b, t, N_HEADS, HEAD_DIM)
    logf_out = lambda lt, b, t: lt.T.reshape(1, b, t, N_HEADS)
    w_keep = min(BAND_WINDOW, sp)
    return (y_prompt, y_sample,
            heads(h_p[1], bp, sp), heads(h_p[2], bp, sp), logf_out(logf_p, bp, sp),
            heads(h_p[4], bp, sp)[:, :, sp - w_keep:], heads(h_p[5], bp, sp)[:, :, sp - w_keep:],
            heads(h_s[1], bs, ts), heads(h_s[2], bs, ts), logf_out(logf_s, bs, ts),
            heads(h_s[4], bs, ts), heads(h_s[5], bs, ts))
```

```python
import functools

import numpy as np
import jax
import jax.numpy as jnp
from jax import lax
from jax.experimental import pallas as pl
from jax.experimental.pallas import tpu as pltpu

D_MODEL = 2048
HEAD_DIM = 128
N_HEADS = 8
W_ATT = N_HEADS * HEAD_DIM
CHUNK = 64
LEFT_CHUNKS = 8
BAND_WINDOW = LEFT_CHUNKS * CHUNK
REL_MAX = 128
N_KEYS = 128
PEER_HEADS = 8
PEER_TOPK = 16
PEER_HALF = 128
PEER_SLOTS = PEER_HEADS * PEER_TOPK
EPS = 1e-6
ATT_SCALE = HEAD_DIM ** -0.5

LANES = 128
SUBLANES = 8
VMEM_LIMIT = 56 * 1024 * 1024
ROW_TILES = D_MODEL // LANES

F32 = jnp.float32
BF16 = jnp.bfloat16
NT_DIMS = (((1,), (1,)), ((), ()))


def _params(*sem):
    return pltpu.CompilerParams(dimension_semantics=sem, vmem_limit_bytes=VMEM_LIMIT)


def _nt(a, b):
    return lax.dot_general(a, b, NT_DIMS, preferred_element_type=F32)


def _norm_mod(x, gain, sc, sh):
    y = x * lax.rsqrt(jnp.mean(x * x, axis=-1, keepdims=True) + EPS)
    return (y * gain) * (1.0 + sc) + sh


def _split_bf16(x):
    hi = x.astype(BF16)
    lo = (x - hi.astype(F32)).astype(BF16)
    return hi, lo


def _adaln_kernel(c_ref, w_ref, b_ref, o_ref):
    c = c_ref[...]
    a = (c * (1.0 / (1.0 + jnp.exp(-c)))).astype(BF16)
    o_ref[...] = jnp.dot(a, w_ref[...].astype(BF16), preferred_element_type=F32) + b_ref[...]


def _adaln(c, w_ada, b_ada):
    rows, tn = c.shape[0], 1024
    n = w_ada.shape[1]
    return pl.pallas_call(
        _adaln_kernel,
        grid=(n // tn,),
        in_specs=[pl.BlockSpec((rows, D_MODEL), lambda j: (0, 0)),
                  pl.BlockSpec((D_MODEL, tn), lambda j: (0, j)),
                  pl.BlockSpec((1, tn), lambda j: (0, j))],
        out_specs=pl.BlockSpec((rows, tn), lambda j: (0, j)),
        out_shape=jax.ShapeDtypeStruct((rows, n), F32),
        compiler_params=_params("arbitrary"),
        name="adaln",
    )(c, w_ada, b_ada.reshape(1, n))


KV_PARTS = (1, 2, 4, 5)


def _kv_slot(j):
    return sum((j >= p).astype(jnp.int32) for p in KV_PARTS[1:])


def _inproj_kernel(x_ref, gain_ref, sc_ref, sh_ref, w_ref, wff_ref, bf_ref, h_ref, logf_ref, kv_ref, xm_scr):
    j = pl.program_id(1)

    @pl.when(j == 0)
    def _():
        xm = _norm_mod(x_ref[...], gain_ref[...], sc_ref[...], sh_ref[...])
        hi, lo = _split_bf16(xm)
        xm_scr[...] = hi
        whi, wlo = _split_bf16(wff_ref[...])
        z = _nt(whi, hi) + _nt(whi, lo) + _nt(wlo, hi) + bf_ref[...]
        logf_ref[...] = jnp.minimum(z, 0.0) - jnp.log1p(jnp.exp(-jnp.abs(z)))

    res = jnp.dot(xm_scr[...], w_ref[...], preferred_element_type=F32)
    h_ref[...] = res

    @pl.when(functools.reduce(jnp.logical_or, [j == p for p in KV_PARTS]))
    def _():
        for hh in range(N_HEADS):
            kv_ref[:, hh, :] = res[:, hh * HEAD_DIM:(hh + 1) * HEAD_DIM]


def _inproj(x2d, gain, sc3, sh3, mod_spec, w_cat, wff_t, bf_col, tm):
    rows = x2d.shape[0]
    tn = W_ATT
    nparts = w_cat.shape[1] // tn
    return pl.pallas_call(
        _inproj_kernel,
        grid=(rows // tm, nparts),
        in_specs=[pl.BlockSpec((tm, D_MODEL), lambda i, j: (i, 0)),
                  pl.BlockSpec((1, D_MODEL), lambda i, j: (0, 0)),
                  mod_spec, mod_spec,
                  pl.BlockSpec((D_MODEL, tn), lambda i, j: (0, j)),
                  pl.BlockSpec((N_HEADS, D_MODEL), lambda i, j: (0, 0)),
                  pl.BlockSpec((N_HEADS, 1), lambda i, j: (0, 0))],
        out_specs=[pl.BlockSpec((None, tm, tn), lambda i, j: (j, i, 0)),
                   pl.BlockSpec((N_HEADS, tm), lambda i, j: (0, i)),
                   pl.BlockSpec((None, tm, N_HEADS, HEAD_DIM), lambda i, j: (_kv_slot(j), i, 0, 0))],
        out_shape=[jax.ShapeDtypeStruct((nparts, rows, tn), F32),
                   jax.ShapeDtypeStruct((N_HEADS, rows), F32),
                   jax.ShapeDtypeStruct((len(KV_PARTS), rows, N_HEADS, HEAD_DIM), F32)],
        scratch_shapes=[pltpu.VMEM((tm, D_MODEL), BF16)],
        compiler_params=_params("arbitrary", "arbitrary"),
        name="inproj",
    )(x2d, gain, sc3, sh3, w_cat, wff_t, bf_col)


CUMSUM_CHUNK = 512


def _cumsum_kernel(x_ref, o_ref):
    c = CUMSUM_CHUNK
    r = lax.broadcasted_iota(jnp.int32, (c, c), 0)
    q = lax.broadcasted_iota(jnp.int32, (c, c), 1)
    tri = jnp.where(r <= q, 1.0, 0.0).astype(BF16)
    carry = jnp.zeros((SUBLANES, 1), F32)
    for k in range(x_ref.shape[1] // c):
        x = x_ref[:, k * c:(k + 1) * c]
        hi = x.astype(BF16)
        r1 = x - hi.astype(F32)
        mid = r1.astype(BF16)
        lo = (r1 - mid.astype(F32)).astype(BF16)
        y = (jnp.dot(hi, tri, preferred_element_type=F32) + jnp.dot(mid, tri, preferred_element_type=F32)
             + jnp.dot(lo, tri, preferred_element_type=F32)) + carry
        o_ref[:, k * c:(k + 1) * c] = y
        carry = y[:, c - 1:c]


def _cumsum_rows(x):
    rows, n = x.shape
    return pl.pallas_call(
        _cumsum_kernel,
        grid=(rows // SUBLANES,),
        in_specs=[pl.BlockSpec((SUBLANES, n), lambda i: (i, 0))],
        out_specs=pl.BlockSpec((SUBLANES, n), lambda i: (i, 0)),
        out_shape=jax.ShapeDtypeStruct((rows, n), F32),
        compiler_params=_params("arbitrary"),
        name="cumsum",
    )(x)


def _softmax_update(s, v, m_prev, l_prev, acc_prev):
    m_new = jnp.maximum(m_prev, jnp.max(s, axis=1, keepdims=True))
    alpha = jnp.exp(m_prev - m_new)
    p = jnp.exp(s - m_new)
    l_new = alpha * l_prev + jnp.sum(p, axis=1, keepdims=True)
    acc_new = alpha * acc_prev + jnp.dot(p.astype(BF16), v, preferred_element_type=F32)
    return m_new, l_new, acc_new


def _fox_prompt_kernel(q_ref, k_ref, v_ref, f_ref, o_ref, m_scr, l_scr, acc_scr, *, tq):
    qi, ki = pl.program_id(2), pl.program_id(3)

    @pl.when(ki == 0)
    def _():
        m_scr[...] = jnp.full(m_scr.shape, -jnp.inf, F32)
        l_scr[...] = jnp.zeros(l_scr.shape, F32)
        acc_scr[...] = jnp.zeros(acc_scr.shape, F32)

    @pl.when(ki <= qi)
    def _():
        s = _nt(q_ref[...].astype(BF16), k_ref[...].astype(BF16)) * ATT_SCALE - f_ref[...]
        row = qi * tq + lax.broadcasted_iota(jnp.int32, s.shape, 0)
        col = ki * tq + lax.broadcasted_iota(jnp.int32, s.shape, 1)
        s = jnp.where(col <= row, s, -jnp.inf)
        m, l, acc = _softmax_update(s, v_ref[...].astype(BF16), m_scr[...], l_scr[...], acc_scr[...])
        m_scr[...], l_scr[...], acc_scr[...] = m, l, acc

    @pl.when(ki == qi)
    def _():
        o_ref[...] = (acc_scr[...] / l_scr[...]).astype(o_ref.dtype)


def _fox_prompt(h, f3, batch, seq, tq=512):
    nq = seq // tq
    rows = batch * seq
    kv = lambda part: pl.BlockSpec(
        (None, tq, HEAD_DIM), lambda b, hh, qi, ki: (part, b * nq + jnp.minimum(ki, qi), hh))
    return pl.pallas_call(
        functools.partial(_fox_prompt_kernel, tq=tq),
        grid=(batch, N_HEADS, nq, nq),
        in_specs=[pl.BlockSpec((None, tq, HEAD_DIM), lambda b, hh, qi, ki: (0, b * nq + qi, hh)),
                  kv(1), kv(2),
                  pl.BlockSpec((None, 1, tq), lambda b, hh, qi, ki: (hh * batch + b, 0, jnp.minimum(ki, qi)))],
        out_specs=pl.BlockSpec((tq, HEAD_DIM), lambda b, hh, qi, ki: (b * nq + qi, hh)),
        out_shape=jax.ShapeDtypeStruct((rows, W_ATT), BF16),
        scratch_shapes=[pltpu.VMEM((tq, 1), F32), pltpu.VMEM((tq, 1), F32), pltpu.VMEM((tq, HEAD_DIM), F32)],
        compiler_params=_params("arbitrary", "arbitrary", "arbitrary", "arbitrary"),
        name="fox_prompt",
    )(h, h, h, f3)


BAND_TQ = BAND_WINDOW


def _band_prompt_kernel(q_ref, ko_ref, kl_ref, vo_ref, vl_ref, bo_ref, bl_ref, o_ref):
    q = q_ref[...].astype(BF16)
    so = _nt(q, ko_ref[...].astype(BF16)) * ATT_SCALE + bo_ref[...]
    sl = _nt(q, kl_ref[...].astype(BF16)) * ATT_SCALE + bl_ref[...]
    sl = jnp.where(pl.program_id(2) > 0, sl, -jnp.inf)
    m = jnp.maximum(jnp.max(so, axis=1, keepdims=True), jnp.max(sl, axis=1, keepdims=True))
    po, pp = jnp.exp(so - m), jnp.exp(sl - m)
    l = jnp.sum(po, axis=1, keepdims=True) + jnp.sum(pp, axis=1, keepdims=True)
    acc = (jnp.dot(po.astype(BF16), vo_ref[...].astype(BF16), preferred_element_type=F32)
           + jnp.dot(pp.astype(BF16), vl_ref[...].astype(BF16), preferred_element_type=F32))
    o_ref[...] = (acc / l).astype(o_ref.dtype)


def _band_prompt(h, bias_own, bias_left, batch, seq):
    tq = BAND_TQ
    nq = seq // tq
    own = lambda part: pl.BlockSpec((None, tq, HEAD_DIM), lambda b, hh, qi: (part, b * nq + qi, hh))
    left = lambda part: pl.BlockSpec(
        (None, tq, HEAD_DIM), lambda b, hh, qi: (part, b * nq + jnp.maximum(qi - 1, 0), hh))
    bias = pl.BlockSpec((None, tq, tq), lambda b, hh, qi: (hh, 0, 0))
    return pl.pallas_call(
        _band_prompt_kernel,
        grid=(batch, N_HEADS, nq),
        in_specs=[own(3), own(4), left(4), own(5), left(5), bias, bias],
        out_specs=pl.BlockSpec((tq, HEAD_DIM), lambda b, hh, qi: (b * nq + qi, hh)),
        out_shape=jax.ShapeDtypeStruct((batch * seq, W_ATT), BF16),
        compiler_params=_params("arbitrary", "arbitrary", "arbitrary"),
        name="band_prompt",
    )(h, h, h, h, h, bias_own, bias_left)


def _rel_bias_toeplitz(rel_bias, n_rows, n_cols, shift):
    hh = rel_bias.shape[0]
    n = n_rows + n_cols - 1
    d = shift + np.arange(n) - (n_cols - 1)
    f = rel_bias[:, np.clip(d, -REL_MAX, REL_MAX) + REL_MAX]
    gp = jnp.pad(f[:, ::-1], ((0, 0), (0, 1)))
    flat = jnp.broadcast_to(gp[:, None, :], (hh, n_rows, n + 1)).reshape(hh, n_rows * (n + 1))
    flat = jnp.pad(flat, ((0, 0), (0, n - n_rows)))
    skew = flat.reshape(hh, n_rows + 1, n)
    return skew[:, :n_rows, n_rows - 1:n_rows - 1 + n_cols].astype(F32)


def _band_prompt_bias(rel_bias):
    r = np.arange(BAND_TQ)[:, None]
    c = np.arange(BAND_TQ)[None, :]
    own = jnp.where((c // CHUNK <= r // CHUNK)[None], _rel_bias_toeplitz(rel_bias, BAND_TQ, BAND_TQ, 0), -jnp.inf)
    left = jnp.where((c // CHUNK >= r // CHUNK)[None],
                     _rel_bias_toeplitz(rel_bias, BAND_TQ, BAND_TQ, BAND_TQ), -jnp.inf)
    return own, left


def _fox_sample_kernel(q_ref, kc_ref, vc_ref, fc_ref, kn_ref, vn_ref, fn_ref, o_ref, m_scr, l_scr, acc_scr,
                       *, nk, t_new):
    ki = pl.program_id(1)

    @pl.when(ki == 0)
    def _():
        m_scr[...] = jnp.full(m_scr.shape, -jnp.inf, F32)
        l_scr[...] = jnp.zeros(l_scr.shape, F32)
        acc_scr[...] = jnp.zeros(acc_scr.shape, F32)

    for hh in range(N_HEADS):
        sl = slice(hh * HEAD_DIM, (hh + 1) * HEAD_DIM)
        q = q_ref[:, sl].astype(BF16)
        s = _nt(q, kc_ref[:, sl].astype(BF16)) * ATT_SCALE - fc_ref[hh:hh + 1, :]
        m, l, acc = _softmax_update(s, vc_ref[:, sl].astype(BF16), m_scr[hh], l_scr[hh], acc_scr[hh])
        m_scr[hh], l_scr[hh], acc_scr[hh] = m, l, acc

    @pl.when(ki == nk - 1)
    def _():
        for hh in range(N_HEADS):
            sl = slice(hh * HEAD_DIM, (hh + 1) * HEAD_DIM)
            q = q_ref[:, sl].astype(BF16)
            s = _nt(q, kn_ref[:, sl].astype(BF16)) * ATT_SCALE - fn_ref[hh:hh + 1, 0:t_new]
            row = lax.broadcasted_iota(jnp.int32, s.shape, 0)
            col = lax.broadcasted_iota(jnp.int32, s.shape, 1)
            s = jnp.where(col <= row, s, -jnp.inf)
            m, l, acc = _softmax_update(s, vn_ref[:, sl].astype(BF16), m_scr[hh], l_scr[hh], acc_scr[hh])
            o_ref[:, sl] = (acc / l).astype(o_ref.dtype)


def _fox_sample(h, cache_k, cache_v, f_cache, f_new, batch, t_new, tk=1024):
    past = cache_k.shape[1]
    nk = past // tk
    new = lambda part: pl.BlockSpec((None, t_new, W_ATT), lambda b, ki: (part, b, 0))
    cache = pl.BlockSpec((None, tk, W_ATT), lambda b, ki: (b, ki, 0))
    return pl.pallas_call(
        functools.partial(_fox_sample_kernel, nk=nk, t_new=t_new),
        grid=(batch, nk),
        in_specs=[new(0), cache, cache,
                  pl.BlockSpec((None, N_HEADS, tk), lambda b, ki: (b, 0, ki)),
                  new(1), new(2),
                  pl.BlockSpec((None, N_HEADS, LANES), lambda b, ki: (b, 0, 0))],
        out_specs=pl.BlockSpec((t_new, W_ATT), lambda b, ki: (b, 0)),
        out_shape=jax.ShapeDtypeStruct((batch * t_new, W_ATT), BF16),
        scratch_shapes=[pltpu.VMEM((N_HEADS, t_new, 1), F32), pltpu.VMEM((N_HEADS, t_new, 1), F32),
                        pltpu.VMEM((N_HEADS, t_new, HEAD_DIM), F32)],
        compiler_params=_params("arbitrary", "arbitrary"),
        name="fox_sample",
    )(h, cache_k, cache_v, f_cache, h, h, f_new)


def _band_sample_kernel(q_ref, kc_ref, vc_ref, kn_ref, vn_ref, bc_ref, bn_ref, o_ref):
    for hh in range(N_HEADS):
        sl = slice(hh * HEAD_DIM, (hh + 1) * HEAD_DIM)
        q = q_ref[:, sl].astype(BF16)
        sc = _nt(q, kc_ref[:, sl].astype(BF16)) * ATT_SCALE + bc_ref[hh]
        sn = _nt(q, kn_ref[:, sl].astype(BF16)) * ATT_SCALE + bn_ref[hh]
        m = jnp.maximum(jnp.max(sc, axis=1, keepdims=True), jnp.max(sn, axis=1, keepdims=True))
        pc, pn = jnp.exp(sc - m), jnp.exp(sn - m)
        l = jnp.sum(pc, axis=1, keepdims=True) + jnp.sum(pn, axis=1, keepdims=True)
        acc = (jnp.dot(pc.astype(BF16), vc_ref[:, sl].astype(BF16), preferred_element_type=F32)
               + jnp.dot(pn.astype(BF16), vn_ref[:, sl].astype(BF16), preferred_element_type=F32))
        o_ref[:, sl] = (acc / l).astype(o_ref.dtype)


def _band_sample(h, cache_k, cache_v, bias_c, bias_n, batch, t_new):
    win = cache_k.shape[1]
    new = lambda part: pl.BlockSpec((None, t_new, W_ATT), lambda b: (part, b, 0))
    cache = pl.BlockSpec((None, win, W_ATT), lambda b: (b, 0, 0))
    return pl.pallas_call(
        _band_sample_kernel,
        grid=(batch,),
        in_specs=[new(3), cache, cache, new(4), new(5),
                  pl.BlockSpec((N_HEADS, t_new, win), lambda b: (0, 0, 0)),
                  pl.BlockSpec((N_HEADS, t_new, t_new), lambda b: (0, 0, 0))],
        out_specs=pl.BlockSpec((t_new, W_ATT), lambda b: (b, 0)),
        out_shape=jax.ShapeDtypeStruct((batch * t_new, W_ATT), BF16),
        compiler_params=_params("arbitrary"),
        name="band_sample",
    )(h, cache_k, cache_v, h, h, bias_c, bias_n)


def _band_sample_bias(rel_bias, win, t_new):
    b = _rel_bias_toeplitz(rel_bias, t_new, win + t_new, win)
    return b[:, :, :win], b[:, :, win:]


def _merge_kernel(ya_ref, yb_ref, ga0_ref, ga1_ref, gb0_ref, gb1_ref, x_ref, g1_ref,
                  wf_ref, wb_ref, wo_ref, o_ref):
    a = jnp.dot(ya_ref[...], wf_ref[...], preferred_element_type=F32)
    b = jnp.dot(yb_ref[...], wb_ref[...], preferred_element_type=F32)
    sig = lambda z: 1.0 / (1.0 + jnp.exp(-z))
    half = D_MODEL // 2
    m0 = (sig(ga0_ref[...]) * a[:, :half] + sig(gb0_ref[...]) * b[:, :half]).astype(BF16)
    m1 = (sig(ga1_ref[...]) * a[:, half:] + sig(gb1_ref[...]) * b[:, half:]).astype(BF16)
    y = (jnp.dot(m0, wo_ref[:half, :], preferred_element_type=F32)
         + jnp.dot(m1, wo_ref[half:, :], preferred_element_type=F32))
    o_ref[...] = x_ref[...] + g1_ref[...] * y


def _merge(ya, yb, h, x2d, g13, mod_spec, wf, wb, wo, tm):
    rows = x2d.shape[0]
    part = lambda p: pl.BlockSpec((None, tm, W_ATT), lambda i: (p, i, 0))
    const = lambda shape: pl.BlockSpec(shape, lambda i: (0, 0), pipeline_mode=pl.Buffered(1))
    return pl.pallas_call(
        _merge_kernel,
        grid=(rows // tm,),
        in_specs=[pl.BlockSpec((tm, W_ATT), lambda i: (i, 0)), pl.BlockSpec((tm, W_ATT), lambda i: (i, 0)),
                  part(6), part(7), part(8), part(9),
                  pl.BlockSpec((tm, D_MODEL), lambda i: (i, 0)),
                  mod_spec,
                  const((W_ATT, D_MODEL)), const((W_ATT, D_MODEL)), const((D_MODEL, D_MODEL))],
        out_specs=pl.BlockSpec((tm, D_MODEL), lambda i: (i, 0)),
        out_shape=jax.ShapeDtypeStruct((rows, D_MODEL), F32),
        compiler_params=_params("arbitrary"),
        name="merge",
    )(ya, yb, h, h, h, h, x2d, g13, wf, wb, wo)


def _peer_query_kernel(x_ref, gain_ref, sc_ref, sh_ref, w_ref, q_ref, xf_ref, xm_scr):
    @pl.when(pl.program_id(1) == 0)
    def _():
        xf = _norm_mod(x_ref[...], gain_ref[...], sc_ref[...], sh_ref[...])
        xf_ref[...] = xf
        xm_scr[...] = xf.astype(BF16)

    q_ref[...] = jnp.dot(xm_scr[...], w_ref[...], preferred_element_type=F32)


def _peer_query(x2d, gain, sc3, sh3, mod_spec, wq, tm):
    rows = x2d.shape[0]
    tn = 1024
    return pl.pallas_call(
        _peer_query_kernel,
        grid=(rows // tm, wq.shape[1] // tn),
        in_specs=[pl.BlockSpec((tm, D_MODEL), lambda i, j: (i, 0)),
                  pl.BlockSpec((1, D_MODEL), lambda i, j: (0, 0)),
                  mod_spec, mod_spec,
                  pl.BlockSpec((D_MODEL, tn), lambda i, j: (0, j))],
        out_specs=[pl.BlockSpec((tm, tn), lambda i, j: (i, j)),
                   pl.BlockSpec((tm, D_MODEL), lambda i, j: (i, 0))],
        out_shape=[jax.ShapeDtypeStruct((rows, wq.shape[1]), F32),
                   jax.ShapeDtypeStruct((rows, D_MODEL), F32)],
        scratch_shapes=[pltpu.VMEM((tm, D_MODEL), BF16)],
        compiler_params=_params("arbitrary", "arbitrary"),
        name="peer_query",
    )(x2d, gain, sc3, sh3, wq)


def _top_k_rows(s, k):
    n = s.shape[0]
    iota = lax.broadcasted_iota(jnp.int32, s.shape, 0)
    vals, idxs = [], []
    for _ in range(k):
        m = jnp.max(s, axis=0, keepdims=True)
        i = jnp.min(jnp.where(s == m, iota, n), axis=0, keepdims=True)
        vals.append(m)
        idxs.append(i)
        s = jnp.where(iota == i, -jnp.inf, s)
    return jnp.concatenate(vals, axis=0), jnp.concatenate(idxs, axis=0)


def _select_rows(table, sel):
    out = jnp.zeros(sel.shape, table.dtype)
    for r in range(table.shape[0]):
        out = jnp.where(sel == r, table[r:r + 1, :], out)
    return out


def _route_kernel(q_ref, sk_ref, e_ref, g_ref):
    k = PEER_TOPK
    e_all, g_all = [], []
    for hh in range(PEER_HEADS):
        halves = []
        for c in range(2):
            col = (2 * hh + c) * PEER_HALF
            qh = q_ref[:, col:col + PEER_HALF].astype(BF16)
            halves.append(_top_k_rows(_nt(sk_ref[hh, c].astype(BF16), qh), k))
        (s1, i1), (s2, i2) = halves
        cand = jnp.concatenate([s1[a:a + 1, :] + s2 for a in range(k)], axis=0)
        top_s, top_c = _top_k_rows(cand, k)
        e_all.append(_select_rows(i1, top_c >> (k.bit_length() - 1)) * N_KEYS + _select_rows(i2, top_c & (k - 1)))
        p = jnp.exp(top_s - top_s[0:1, :])
        g_all.append(p / jnp.sum(p, axis=0, keepdims=True))
    e_ref[...] = jnp.concatenate(e_all, axis=0).T
    g_ref[...] = jnp.concatenate(g_all, axis=0).T


def _route(qp, sub_keys, tt):
    rows = qp.shape[0]
    return pl.pallas_call(
        _route_kernel,
        grid=(rows // tt,),
        in_specs=[pl.BlockSpec((tt, qp.shape[1]), lambda i: (i, 0)),
                  pl.BlockSpec(sub_keys.shape, lambda i: (0, 0, 0, 0))],
        out_specs=[pl.BlockSpec((tt, PEER_SLOTS), lambda i: (i, 0)),
                   pl.BlockSpec((tt, PEER_SLOTS), lambda i: (i, 0))],
        out_shape=[jax.ShapeDtypeStruct((rows, PEER_SLOTS), jnp.int32),
                   jax.ShapeDtypeStruct((rows, PEER_SLOTS), F32)],
        compiler_params=_params("arbitrary"),
        name="peer_route",
    )(qp, sub_keys)


PEER_TOK = 4
UV_ROWS = 2 * ROW_TILES
PEER_GROUP = SUBLANES
PEER_NGROUP = PEER_SLOTS // PEER_GROUP


def _peer_kernel(idx_ref, g_ref, xf_ref, x1_ref, g2_ref, nf_ref, uv_ref, y_ref,
                 buf_a, buf_b, p_scr, wb_scr, xf3_scr, o3_scr, sem, *, tb):
    n_stage = tb // PEER_TOK

    def issue(tok, dst, half, k, j):
        e = idx_ref[0, tok * PEER_SLOTS + j]
        pltpu.make_async_copy(uv_ref.at[e], dst.at[k, j], sem.at[half, k]).start()

    def wait(dst, half, k):
        pltpu.make_async_copy(uv_ref.at[pl.ds(0, PEER_SLOTS)], dst.at[k], sem.at[half, k]).wait()

    def prologue(jo, c):
        for ji in range(PEER_GROUP):
            for k in range(PEER_TOK):
                issue(k, buf_a, 0, k, jo * PEER_GROUP + ji)
        return c
    lax.fori_loop(0, PEER_NGROUP, prologue, 0)

    for c in range(ROW_TILES):
        xf3_scr[:, c, :] = xf_ref[:, c * LANES:(c + 1) * LANES]

    def stage(i, half, prefetch):
        buf, nxt = (buf_a, buf_b) if half == 0 else (buf_b, buf_a)
        t0 = i * PEER_TOK
        for k in range(PEER_TOK):
            wait(buf, half, k)
        xs = [(xf3_scr[t0 + k, 0:SUBLANES, :], xf3_scr[t0 + k, SUBLANES:ROW_TILES, :])
              for k in range(PEER_TOK)]

        def dot_body(jo, c):
            for ji in range(PEER_GROUP):
                j = jo * PEER_GROUP + ji
                if prefetch:
                    issue(t0 + PEER_TOK, nxt, 1 - half, 0, j)
                    issue(t0 + PEER_TOK + 1, nxt, 1 - half, 1, j)
                row = pl.multiple_of(j * SUBLANES, SUBLANES)
                for k in range(PEER_TOK):
                    p = buf[k, j, 0:SUBLANES, :] * xs[k][0] + buf[k, j, SUBLANES:ROW_TILES, :] * xs[k][1]
                    p_scr[k, pl.ds(row, SUBLANES), :] = p
            return c
        lax.fori_loop(0, PEER_NGROUP, dot_body, 0)

        for k in range(PEER_TOK):
            ps = p_scr[k, pl.ds(0, PEER_SLOTS, stride=SUBLANES), :]
            for s in range(1, SUBLANES):
                ps = ps + p_scr[k, pl.ds(s, PEER_SLOTS, stride=SUBLANES), :]
            h = jnp.sum(ps.T, axis=0, keepdims=True)
            w = g_ref[pl.ds(t0 + k, 1), :] * (0.5 * h * (1.0 + lax.erf(h * (2.0 ** -0.5))))
            wb_scr[k] = jnp.broadcast_to(w, (PEER_SLOTS, PEER_SLOTS)).T

        def acc_body(jo, accs):
            accs = list(accs)
            for ji in range(PEER_GROUP):
                j = jo * PEER_GROUP + ji
                if prefetch:
                    issue(t0 + PEER_TOK + 2, nxt, 1 - half, 2, j)
                    issue(t0 + PEER_TOK + 3, nxt, 1 - half, 3, j)
                for k in range(PEER_TOK):
                    wv = wb_scr[k, pl.ds(j, 1), :]
                    accs[2 * k] = accs[2 * k] + buf[k, j, ROW_TILES:ROW_TILES + SUBLANES, :] * wv
                    accs[2 * k + 1] = accs[2 * k + 1] + buf[k, j, ROW_TILES + SUBLANES:UV_ROWS, :] * wv
            return tuple(accs)
        zero = jnp.zeros((SUBLANES, LANES), F32)
        accs = lax.fori_loop(0, PEER_NGROUP, acc_body, (zero,) * (2 * PEER_TOK))
        for k in range(PEER_TOK):
            o3_scr[t0 + k, 0:SUBLANES, :] = accs[2 * k]
            o3_scr[t0 + k, SUBLANES:ROW_TILES, :] = accs[2 * k + 1]

    def stage_pair(ip, c):
        stage(2 * ip, 0, True)
        stage(2 * ip + 1, 1, True)
        return c
    lax.fori_loop(0, n_stage // 2 - 1, stage_pair, 0)
    stage(n_stage - 2, 0, True)
    stage(n_stage - 1, 1, False)

    ss = jnp.zeros((tb, 1), F32)
    for c in range(ROW_TILES):
        sl = slice(c * LANES, (c + 1) * LANES)
        z = x1_ref[:, sl] + g2_ref[:, sl] * o3_scr[:, c, :]
        y_ref[:, sl] = z
        ss = ss + jnp.sum(z * z, axis=1, keepdims=True)
    inv = lax.rsqrt(ss * (1.0 / D_MODEL) + EPS)
    for c in range(ROW_TILES):
        sl = slice(c * LANES, (c + 1) * LANES)
        y_ref[:, sl] = y_ref[:, sl] * inv * nf_ref[:, sl]


def _peer(idx, gate, xf, x1, g23, g2_spec, nf, uv, tb):
    rows = xf.shape[0]
    nblk = rows // tb
    tok = lambda: pl.BlockSpec((tb, D_MODEL), lambda i: (i, 0))
    return pl.pallas_call(
        functools.partial(_peer_kernel, tb=tb),
        grid=(nblk,),
        in_specs=[pl.BlockSpec((None, 1, tb * PEER_SLOTS), lambda i: (i, 0, 0), memory_space=pltpu.SMEM),
                  pl.BlockSpec((tb, PEER_SLOTS), lambda i: (i, 0)),
                  tok(), tok(), g2_spec,
                  pl.BlockSpec((1, D_MODEL), lambda i: (0, 0)),
                  pl.BlockSpec(memory_space=pl.ANY)],
        out_specs=tok(),
        out_shape=jax.ShapeDtypeStruct((rows, D_MODEL), F32),
        scratch_shapes=[pltpu.VMEM((PEER_TOK, PEER_SLOTS, UV_ROWS, LANES), F32),
                        pltpu.VMEM((PEER_TOK, PEER_SLOTS, UV_ROWS, LANES), F32),
                        pltpu.VMEM((PEER_TOK, PEER_SLOTS * SUBLANES, LANES), F32),
                        pltpu.VMEM((PEER_TOK, PEER_SLOTS, LANES), F32),
                        pltpu.VMEM((tb, ROW_TILES, LANES), F32),
                        pltpu.VMEM((tb, ROW_TILES, LANES), F32),
                        pltpu.SemaphoreType.DMA((2, PEER_TOK))],
        compiler_params=_params("arbitrary"),
        name="peer_experts",
    )(idx.reshape(nblk, 1, tb * PEER_SLOTS), gate, xf, x1, g23, nf, uv)


def _stream(x, mods, weights, attend, tm, tb):
    batch, seq, _ = x.shape
    rows = batch * seq
    tm = min(tm, rows)
    sh1, sc1, g1, sh2, sc2, g2 = mods
    x2d = x.reshape(rows, D_MODEL)
    if seq >= tm:
        per = seq // tm
        mod3 = lambda m: m.reshape(batch, 1, D_MODEL)
        mod_spec2 = pl.BlockSpec((None, 1, D_MODEL), lambda i, j: (i // per, 0, 0))
        mod_spec1 = pl.BlockSpec((None, 1, D_MODEL), lambda i: (i // per, 0, 0))
    else:
        mod3 = lambda m: jnp.repeat(m, seq, axis=0).reshape(rows // tm, tm, D_MODEL)
        mod_spec2 = pl.BlockSpec((None, tm, D_MODEL), lambda i, j: (i, 0, 0))
        mod_spec1 = pl.BlockSpec((None, tm, D_MODEL), lambda i: (i, 0, 0))

    h, logf_t, kv = _inproj(x2d, weights["norm_mix"], mod3(sc1), mod3(sh1), mod_spec2,
                            weights["w_cat"], weights["wff_t"], weights["bf_col"], tm)
    ya, yb = attend(h, logf_t)
    tm2 = min(tm, 256)
    if seq >= tm2:
        per2 = seq // tm2
        g13 = g1.reshape(batch, 1, D_MODEL)
        g1_spec = pl.BlockSpec((None, 1, D_MODEL), lambda i: (i // per2, 0, 0))
    else:
        g13 = jnp.repeat(g1, seq, axis=0).reshape(rows // tm2, tm2, D_MODEL)
        g1_spec = pl.BlockSpec((None, tm2, D_MODEL), lambda i: (i, 0, 0))
    x1 = _merge(ya, yb, h, x2d, g13, g1_spec, weights["wf"], weights["wb"], weights["wo"], tm2)
    qp, xf = _peer_query(x1, weights["norm_ffn"], mod3(sc2), mod3(sh2), mod_spec2, weights["wq"], tm)
    idx, gate = _route(qp, weights["sub_keys"], 256)
    per_tb = seq // tb
    g2_spec = pl.BlockSpec((None, 1, D_MODEL), lambda i: (i // per_tb, 0, 0))
    y = _peer(idx, gate, xf, x1, g2.reshape(batch, 1, D_MODEL), g2_spec, weights["nf"], weights["uv"], tb)
    return y.reshape(batch, seq, D_MODEL), h, kv, logf_t


def kernel(x_prompt, x_sample, c_prompt, c_sample, cache_fox_k, cache_fox_v, cache_fox_logf, cache_band_k, cache_band_v, w_ada, b_ada, norm_mix, norm_ffn, w_in, b_forget, rel_bias, w_branch_fox, w_branch_band, w_out, w_query, sub_keys, expert_u, expert_v, norm_final):
    bp, sp, _ = x_prompt.shape
    bs, ts, _ = x_sample.shape
    past = cache_fox_k.shape[2]
    win = cache_band_k.shape[2]
    n_exp = expert_u.shape[1]

    w = w_in[0]
    o_ff = 3 * W_ATT
    weights = {
        "w_cat": jnp.concatenate([w[:, :o_ff], w[:, o_ff + N_HEADS:]], axis=1).astype(BF16),
        "wff_t": w[:, o_ff:o_ff + N_HEADS].T,
        "bf_col": b_forget[0].reshape(N_HEADS, 1),
        "norm_mix": norm_mix[0].reshape(1, D_MODEL),
        "norm_ffn": norm_ffn[0].reshape(1, D_MODEL),
        "wf": w_branch_fox[0].astype(BF16),
        "wb": w_branch_band[0].astype(BF16),
        "wo": w_out[0].astype(BF16),
        "wq": w_query[0].astype(BF16),
        "sub_keys": sub_keys[0],
        "nf": norm_final.reshape(1, D_MODEL),
        "uv": jnp.concatenate([expert_u[0].reshape(n_exp, ROW_TILES, LANES),
                               expert_v[0].reshape(n_exp, ROW_TILES, LANES)], axis=1),
    }
    rb = rel_bias[0]
    bias_own, bias_left = _band_prompt_bias(rb)
    bias_c, bias_n = _band_sample_bias(rb, win, ts)

    ada = _adaln(jnp.concatenate([c_prompt, c_sample], axis=0), w_ada[0], b_ada[0])
    mods_p = [ada[:bp, i * D_MODEL:(i + 1) * D_MODEL] for i in range(6)]
    mods_s = [ada[bp:, i * D_MODEL:(i + 1) * D_MODEL] for i in range(6)]

    def attend_prompt(h, logf_t):
        f = _cumsum_rows(logf_t.reshape(N_HEADS * bp, sp))
        ya = _fox_prompt(h, f.reshape(N_HEADS * bp, 1, sp), bp, sp)
        yb = _band_prompt(h, bias_own, bias_left, bp, sp)
        return ya, yb

    y_prompt, _, kv_p, logf_p = _stream(x_prompt, mods_p, weights, attend_prompt, tm=512, tb=64)

    kc = cache_fox_k[0].reshape(bs, past, W_ATT)
    vc = cache_fox_v[0].reshape(bs, past, W_ATT)
    bkc = cache_band_k[0].reshape(bs, win, W_ATT)
    bvc = cache_band_v[0].reshape(bs, win, W_ATT)

    def attend_sample(h, logf_t):
        lf_new = logf_t.reshape(N_HEADS, bs, ts).transpose(1, 0, 2)
        lf_all = jnp.concatenate([cache_fox_logf[0].transpose(0, 2, 1), lf_new], axis=2)
        n_pad = -(past + ts) % CUMSUM_CHUNK
        lf_all = jnp.pad(lf_all, ((0, 0), (0, 0), (0, n_pad)))
        f = _cumsum_rows(lf_all.reshape(bs * N_HEADS, -1)).reshape(bs, N_HEADS, -1)
        ya = _fox_sample(h, kc, vc, f[:, :, :past], f[:, :, past:past + LANES], bs, ts)
        yb = _band_sample(h, bkc, bvc, bias_c, bias_n, bs, ts)
        return ya, yb

    y_sample, _, kv_s, logf_s = _stream(x_sample, mods_s, weights, attend_sample, tm=512, tb=ts)

    heads = lambda part, b, t: part.reshape(1, b, t, N_HEADS, HEAD_DIM)
    logf_out = lambda lt, b, t: lt.T.reshape(1, b, t, N_HEADS)
    w_keep = min(BAND_WINDOW, sp)
    return (y_prompt, y_sample,
            heads(kv_p[0], bp, sp), heads(kv_p[1], bp, sp), logf_out(logf_p, bp, sp),
            heads(kv_p[2], bp, sp)[:, :, sp - w_keep:], heads(kv_p[3], bp, sp)[:, :, sp - w_keep:],
            heads(kv_s[0], bs, ts), heads(kv_s[1], bs, ts), logf_out(logf_s, bs, ts),
            heads(kv_s[2], bs, ts), heads(kv_s[3], bs, ts))
```

```python
import functools

import numpy as np
import jax
import jax.numpy as jnp
from jax import lax
from jax.experimental import pallas as pl
from jax.experimental.pallas import tpu as pltpu

D_MODEL = 2048
HEAD_DIM = 128
N_HEADS = 8
W_ATT = N_HEADS * HEAD_DIM
CHUNK = 64
LEFT_CHUNKS = 8
BAND_WINDOW = LEFT_CHUNKS * CHUNK
REL_MAX = 128
N_KEYS = 128
PEER_HEADS = 8
PEER_TOPK = 16
PEER_HALF = 128
PEER_SLOTS = PEER_HEADS * PEER_TOPK
EPS = 1e-6
ATT_SCALE = HEAD_DIM ** -0.5

LANES = 128
SUBLANES = 8
VMEM_LIMIT = 56 * 1024 * 1024
ROW_TILES = D_MODEL // LANES

F32 = jnp.float32
BF16 = jnp.bfloat16
NT_DIMS = (((1,), (1,)), ((), ()))


def _params(*sem):
    return pltpu.CompilerParams(dimension_semantics=sem, vmem_limit_bytes=VMEM_LIMIT)


def _nt(a, b):
    return lax.dot_general(a, b, NT_DIMS, preferred_element_type=F32)


def _norm_mod(x, gain, sc, sh):
    y = x * lax.rsqrt(jnp.mean(x * x, axis=-1, keepdims=True) + EPS)
    return (y * gain) * (1.0 + sc) + sh


def _split_bf16(x):
    hi = x.astype(BF16)
    lo = (x - hi.astype(F32)).astype(BF16)
    return hi, lo


def _adaln_kernel(c_ref, w_ref, b_ref, o_ref):
    c = c_ref[...]
    a = (c * (1.0 / (1.0 + jnp.exp(-c)))).astype(BF16)
    o_ref[...] = jnp.dot(a, w_ref[...].astype(BF16), preferred_element_type=F32) + b_ref[...]


def _adaln(c, w_ada, b_ada):
    rows, tn = c.shape[0], 1024
    n = w_ada.shape[1]
    return pl.pallas_call(
        _adaln_kernel,
        grid=(n // tn,),
        in_specs=[pl.BlockSpec((rows, D_MODEL), lambda j: (0, 0)),
                  pl.BlockSpec((D_MODEL, tn), lambda j: (0, j)),
                  pl.BlockSpec((1, tn), lambda j: (0, j))],
        out_specs=pl.BlockSpec((rows, tn), lambda j: (0, j)),
        out_shape=jax.ShapeDtypeStruct((rows, n), F32),
        compiler_params=_params("arbitrary"),
        name="adaln",
    )(c, w_ada, b_ada.reshape(1, n))


KV_PARTS = (1, 2, 4, 5)


def _kv_slot(j):
    return sum((j >= p).astype(jnp.int32) for p in KV_PARTS[1:])


def _inproj_kernel(x_ref, gain_ref, sc_ref, sh_ref, w_ref, wff_ref, bf_ref, h_ref, logf_ref, kv_ref, xm_scr):
    j = pl.program_id(1)

    @pl.when(j == 0)
    def _():
        xm = _norm_mod(x_ref[...], gain_ref[...], sc_ref[...], sh_ref[...])
        hi, lo = _split_bf16(xm)
        xm_scr[...] = hi
        whi, wlo = _split_bf16(wff_ref[...])
        z = _nt(whi, hi) + _nt(whi, lo) + _nt(wlo, hi) + bf_ref[...]
        logf_ref[...] = jnp.minimum(z, 0.0) - jnp.log1p(jnp.exp(-jnp.abs(z)))

    res = jnp.dot(xm_scr[...], w_ref[...], preferred_element_type=F32)
    h_ref[...] = res

    @pl.when(functools.reduce(jnp.logical_or, [j == p for p in KV_PARTS]))
    def _():
        for hh in range(N_HEADS):
            kv_ref[:, hh, :] = res[:, hh * HEAD_DIM:(hh + 1) * HEAD_DIM]


def _inproj(x2d, gain, sc3, sh3, mod_spec, w_cat, wff_t, bf_col, tm):
    rows = x2d.shape[0]
    tn = W_ATT
    nparts = w_cat.shape[1] // tn
    return pl.pallas_call(
        _inproj_kernel,
        grid=(rows // tm, nparts),
        in_specs=[pl.BlockSpec((tm, D_MODEL), lambda i, j: (i, 0)),
                  pl.BlockSpec((1, D_MODEL), lambda i, j: (0, 0)),
                  mod_spec, mod_spec,
                  pl.BlockSpec((D_MODEL, tn), lambda i, j: (0, j)),
                  pl.BlockSpec((N_HEADS, D_MODEL), lambda i, j: (0, 0)),
                  pl.BlockSpec((N_HEADS, 1), lambda i, j: (0, 0))],
        out_specs=[pl.BlockSpec((None, tm, tn), lambda i, j: (j, i, 0)),
                   pl.BlockSpec((N_HEADS, tm), lambda i, j: (0, i)),
                   pl.BlockSpec((None, tm, N_HEADS, HEAD_DIM), lambda i, j: (_kv_slot(j), i, 0, 0))],
        out_shape=[jax.ShapeDtypeStruct((nparts, rows, tn), F32),
                   jax.ShapeDtypeStruct((N_HEADS, rows), F32),
                   jax.ShapeDtypeStruct((len(KV_PARTS), rows, N_HEADS, HEAD_DIM), F32)],
        scratch_shapes=[pltpu.VMEM((tm, D_MODEL), BF16)],
        compiler_params=_params("arbitrary", "arbitrary"),
        name="inproj",
    )(x2d, gain, sc3, sh3, w_cat, wff_t, bf_col)


CUMSUM_CHUNK = 512


def _cumsum_kernel(x_ref, o_ref):
    c = CUMSUM_CHUNK
    r = lax.broadcasted_iota(jnp.int32, (c, c), 0)
    q = lax.broadcasted_iota(jnp.int32, (c, c), 1)
    tri = jnp.where(r <= q, 1.0, 0.0).astype(BF16)
    carry = jnp.zeros((SUBLANES, 1), F32)
    for k in range(x_ref.shape[1] // c):
        x = x_ref[:, k * c:(k + 1) * c]
        hi = x.astype(BF16)
        r1 = x - hi.astype(F32)
        mid = r1.astype(BF16)
        lo = (r1 - mid.astype(F32)).astype(BF16)
        y = (jnp.dot(hi, tri, preferred_element_type=F32) + jnp.dot(mid, tri, preferred_element_type=F32)
             + jnp.dot(lo, tri, preferred_element_type=F32)) + carry
        o_ref[:, k * c:(k + 1) * c] = y
        carry = y[:, c - 1:c]


def _cumsum_rows(x):
    rows, n = x.shape
    return pl.pallas_call(
        _cumsum_kernel,
        grid=(rows // SUBLANES,),
        in_specs=[pl.BlockSpec((SUBLANES, n), lambda i: (i, 0))],
        out_specs=pl.BlockSpec((SUBLANES, n), lambda i: (i, 0)),
        out_shape=jax.ShapeDtypeStruct((rows, n), F32),
        compiler_params=_params("arbitrary"),
        name="cumsum",
    )(x)


def _softmax_update(s, v, m_prev, l_prev, acc_prev):
    m_new = jnp.maximum(m_prev, jnp.max(s, axis=1, keepdims=True))
    alpha = jnp.exp(m_prev - m_new)
    p = jnp.exp(s - m_new)
    l_new = alpha * l_prev + jnp.sum(p, axis=1, keepdims=True)
    acc_new = alpha * acc_prev + jnp.dot(p.astype(BF16), v, preferred_element_type=F32)
    return m_new, l_new, acc_new


def _fox_prompt_kernel(q_ref, k_ref, v_ref, f_ref, o_ref, m_scr, l_scr, acc_scr, *, tq):
    qi, ki = pl.program_id(2), pl.program_id(3)

    @pl.when(ki == 0)
    def _():
        m_scr[...] = jnp.full(m_scr.shape, -jnp.inf, F32)
        l_scr[...] = jnp.zeros(l_scr.shape, F32)
        acc_scr[...] = jnp.zeros(acc_scr.shape, F32)

    @pl.when(ki <= qi)
    def _():
        s = _nt(q_ref[...].astype(BF16), k_ref[...].astype(BF16)) * ATT_SCALE - f_ref[...]
        row = qi * tq + lax.broadcasted_iota(jnp.int32, s.shape, 0)
        col = ki * tq + lax.broadcasted_iota(jnp.int32, s.shape, 1)
        s = jnp.where(col <= row, s, -jnp.inf)
        m, l, acc = _softmax_update(s, v_ref[...].astype(BF16), m_scr[...], l_scr[...], acc_scr[...])
        m_scr[...], l_scr[...], acc_scr[...] = m, l, acc

    @pl.when(ki == qi)
    def _():
        o_ref[...] = (acc_scr[...] / l_scr[...]).astype(o_ref.dtype)


def _fox_prompt(h, f3, batch, seq, tq=512):
    nq = seq // tq
    rows = batch * seq
    kv = lambda part: pl.BlockSpec(
        (None, tq, HEAD_DIM), lambda b, hh, qi, ki: (part, b * nq + jnp.minimum(ki, qi), hh))
    return pl.pallas_call(
        functools.partial(_fox_prompt_kernel, tq=tq),
        grid=(batch, N_HEADS, nq, nq),
        in_specs=[pl.BlockSpec((None, tq, HEAD_DIM), lambda b, hh, qi, ki: (0, b * nq + qi, hh)),
                  kv(1), kv(2),
                  pl.BlockSpec((None, 1, tq), lambda b, hh, qi, ki: (hh * batch + b, 0, jnp.minimum(ki, qi)))],
        out_specs=pl.BlockSpec((tq, HEAD_DIM), lambda b, hh, qi, ki: (b * nq + qi, hh)),
        out_shape=jax.ShapeDtypeStruct((rows, W_ATT), BF16),
        scratch_shapes=[pltpu.VMEM((tq, 1), F32), pltpu.VMEM((tq, 1), F32), pltpu.VMEM((tq, HEAD_DIM), F32)],
        compiler_params=_params("arbitrary", "arbitrary", "arbitrary", "arbitrary"),
        name="fox_prompt",
    )(h, h, h, f3)


BAND_TQ = BAND_WINDOW


def _band_prompt_kernel(q_ref, ko_ref, kl_ref, vo_ref, vl_ref, bo_ref, bl_ref, o_ref):
    q = q_ref[...].astype(BF16)
    so = _nt(q, ko_ref[...].astype(BF16)) * ATT_SCALE + bo_ref[...]
    sl = _nt(q, kl_ref[...].astype(BF16)) * ATT_SCALE + bl_ref[...]
    sl = jnp.where(pl.program_id(2) > 0, sl, -jnp.inf)
    m = jnp.maximum(jnp.max(so, axis=1, keepdims=True), jnp.max(sl, axis=1, keepdims=True))
    po, pp = jnp.exp(so - m), jnp.exp(sl - m)
    l = jnp.sum(po, axis=1, keepdims=True) + jnp.sum(pp, axis=1, keepdims=True)
    acc = (jnp.dot(po.astype(BF16), vo_ref[...].astype(BF16), preferred_element_type=F32)
           + jnp.dot(pp.astype(BF16), vl_ref[...].astype(BF16), preferred_element_type=F32))
    o_ref[...] = (acc / l).astype(o_ref.dtype)


def _band_prompt(h, bias_own, bias_left, batch, seq):
    tq = BAND_TQ
    nq = seq // tq
    own = lambda part: pl.BlockSpec((None, tq, HEAD_DIM), lambda b, hh, qi: (part, b * nq + qi, hh))
    left = lambda part: pl.BlockSpec(
        (None, tq, HEAD_DIM), lambda b, hh, qi: (part, b * nq + jnp.maximum(qi - 1, 0), hh))
    bias = pl.BlockSpec((None, tq, tq), lambda b, hh, qi: (hh, 0, 0))
    return pl.pallas_call(
        _band_prompt_kernel,
        grid=(batch, N_HEADS, nq),
        in_specs=[own(3), own(4), left(4), own(5), left(5), bias, bias],
        out_specs=pl.BlockSpec((tq, HEAD_DIM), lambda b, hh, qi: (b * nq + qi, hh)),
        out_shape=jax.ShapeDtypeStruct((batch * seq, W_ATT), BF16),
        compiler_params=_params("arbitrary", "arbitrary", "arbitrary"),
        name="band_prompt",
    )(h, h, h, h, h, bias_own, bias_left)


def _skew_kernel(f_ref, o_ref):
    fb = jnp.broadcast_to(f_ref[...], o_ref.shape)
    o_ref[...] = pltpu.roll(fb, 0, 1, stride=1, stride_axis=0)


def _rel_bias_toeplitz(rel_bias, n_rows, n_cols, shift):
    hh = rel_bias.shape[0]
    w = -(-(n_rows + n_cols - 1) // LANES) * LANES
    m = np.arange(w)
    m = np.where(m < n_cols, m, m - w)
    f = rel_bias[:, np.clip(shift - m, -REL_MAX, REL_MAX) + REL_MAX].astype(F32)
    table = pl.pallas_call(
        _skew_kernel,
        grid=(hh,),
        in_specs=[pl.BlockSpec((None, 1, w), lambda h: (h, 0, 0))],
        out_specs=pl.BlockSpec((None, n_rows, w), lambda h: (h, 0, 0)),
        out_shape=jax.ShapeDtypeStruct((hh, n_rows, w), F32),
        compiler_params=_params("arbitrary"),
        name="rel_bias_table",
    )(f.reshape(hh, 1, w))
    return table[:, :, :n_cols]


def _band_prompt_bias(rel_bias):
    r = np.arange(BAND_TQ)[:, None]
    c = np.arange(BAND_TQ)[None, :]
    own = jnp.where((c // CHUNK <= r // CHUNK)[None], _rel_bias_toeplitz(rel_bias, BAND_TQ, BAND_TQ, 0), -jnp.inf)
    left = jnp.where((c // CHUNK >= r // CHUNK)[None],
                     _rel_bias_toeplitz(rel_bias, BAND_TQ, BAND_TQ, BAND_TQ), -jnp.inf)
    return own, left


def _fox_sample_kernel(q_ref, kc_ref, vc_ref, fc_ref, kn_ref, vn_ref, fn_ref, o_ref, m_scr, l_scr, acc_scr,
                       *, nk, t_new):
    ki = pl.program_id(1)

    @pl.when(ki == 0)
    def _():
        m_scr[...] = jnp.full(m_scr.shape, -jnp.inf, F32)
        l_scr[...] = jnp.zeros(l_scr.shape, F32)
        acc_scr[...] = jnp.zeros(acc_scr.shape, F32)

    for hh in range(N_HEADS):
        sl = slice(hh * HEAD_DIM, (hh + 1) * HEAD_DIM)
        q = q_ref[:, sl].astype(BF16)
        s = _nt(q, kc_ref[:, hh, :].astype(BF16)) * ATT_SCALE - fc_ref[hh:hh + 1, :]
        m, l, acc = _softmax_update(s, vc_ref[:, hh, :].astype(BF16), m_scr[hh], l_scr[hh], acc_scr[hh])
        m_scr[hh], l_scr[hh], acc_scr[hh] = m, l, acc

    @pl.when(ki == nk - 1)
    def _():
        for hh in range(N_HEADS):
            sl = slice(hh * HEAD_DIM, (hh + 1) * HEAD_DIM)
            q = q_ref[:, sl].astype(BF16)
            s = _nt(q, kn_ref[:, sl].astype(BF16)) * ATT_SCALE - fn_ref[hh:hh + 1, 0:t_new]
            row = lax.broadcasted_iota(jnp.int32, s.shape, 0)
            col = lax.broadcasted_iota(jnp.int32, s.shape, 1)
            s = jnp.where(col <= row, s, -jnp.inf)
            m, l, acc = _softmax_update(s, vn_ref[:, sl].astype(BF16), m_scr[hh], l_scr[hh], acc_scr[hh])
            o_ref[:, sl] = (acc / l).astype(o_ref.dtype)


def _fox_sample(h, cache_k, cache_v, f_cache, f_new, batch, t_new, tk=1024):
    past = cache_k.shape[2]
    nk = past // tk
    new = lambda part: pl.BlockSpec((None, t_new, W_ATT), lambda b, ki: (part, b, 0))
    cache = pl.BlockSpec((None, None, tk, N_HEADS, HEAD_DIM), lambda b, ki: (0, b, ki, 0, 0))
    return pl.pallas_call(
        functools.partial(_fox_sample_kernel, nk=nk, t_new=t_new),
        grid=(batch, nk),
        in_specs=[new(0), cache, cache,
                  pl.BlockSpec((None, N_HEADS, tk), lambda b, ki: (b, 0, ki)),
                  new(1), new(2),
                  pl.BlockSpec((None, N_HEADS, LANES), lambda b, ki: (b, 0, 0))],
        out_specs=pl.BlockSpec((t_new, W_ATT), lambda b, ki: (b, 0)),
        out_shape=jax.ShapeDtypeStruct((batch * t_new, W_ATT), BF16),
        scratch_shapes=[pltpu.VMEM((N_HEADS, t_new, 1), F32), pltpu.VMEM((N_HEADS, t_new, 1), F32),
                        pltpu.VMEM((N_HEADS, t_new, HEAD_DIM), F32)],
        compiler_params=_params("arbitrary", "arbitrary"),
        name="fox_sample",
    )(h, cache_k, cache_v, f_cache, h, h, f_new)


def _band_sample_kernel(q_ref, kc_ref, vc_ref, kn_ref, vn_ref, bc_ref, bn_ref, o_ref):
    for hh in range(N_HEADS):
        sl = slice(hh * HEAD_DIM, (hh + 1) * HEAD_DIM)
        q = q_ref[:, sl].astype(BF16)
        sc = _nt(q, kc_ref[:, hh, :].astype(BF16)) * ATT_SCALE + bc_ref[hh]
        sn = _nt(q, kn_ref[:, sl].astype(BF16)) * ATT_SCALE + bn_ref[hh]
        m = jnp.maximum(jnp.max(sc, axis=1, keepdims=True), jnp.max(sn, axis=1, keepdims=True))
        pc, pn = jnp.exp(sc - m), jnp.exp(sn - m)
        l = jnp.sum(pc, axis=1, keepdims=True) + jnp.sum(pn, axis=1, keepdims=True)
        acc = (jnp.dot(pc.astype(BF16), vc_ref[:, hh, :].astype(BF16), preferred_element_type=F32)
               + jnp.dot(pn.astype(BF16), vn_ref[:, sl].astype(BF16), preferred_element_type=F32))
        o_ref[:, sl] = (acc / l).astype(o_ref.dtype)


def _band_sample(h, cache_k, cache_v, bias_c, bias_n, batch, t_new):
    win = cache_k.shape[2]
    new = lambda part: pl.BlockSpec((None, t_new, W_ATT), lambda b: (part, b, 0))
    cache = pl.BlockSpec((None, None, win, N_HEADS, HEAD_DIM), lambda b: (0, b, 0, 0, 0))
    return pl.pallas_call(
        _band_sample_kernel,
        grid=(batch,),
        in_specs=[new(3), cache, cache, new(4), new(5),
                  pl.BlockSpec((N_HEADS, t_new, win), lambda b: (0, 0, 0)),
                  pl.BlockSpec((N_HEADS, t_new, t_new), lambda b: (0, 0, 0))],
        out_specs=pl.BlockSpec((t_new, W_ATT), lambda b: (b, 0)),
        out_shape=jax.ShapeDtypeStruct((batch * t_new, W_ATT), BF16),
        compiler_params=_params("arbitrary"),
        name="band_sample",
    )(h, cache_k, cache_v, h, h, bias_c, bias_n)


def _band_sample_bias(rel_bias, win, t_new):
    b = _rel_bias_toeplitz(rel_bias, t_new, win + t_new, win)
    return b[:, :, :win], b[:, :, win:]


def _merge_kernel(ya_ref, yb_ref, ga0_ref, ga1_ref, gb0_ref, gb1_ref, x_ref, g1_ref,
                  wf_ref, wb_ref, wo_ref, o_ref):
    a = jnp.dot(ya_ref[...], wf_ref[...], preferred_element_type=F32)
    b = jnp.dot(yb_ref[...], wb_ref[...], preferred_element_type=F32)
    sig = lambda z: 1.0 / (1.0 + jnp.exp(-z))
    half = D_MODEL // 2
    m0 = (sig(ga0_ref[...]) * a[:, :half] + sig(gb0_ref[...]) * b[:, :half]).astype(BF16)
    m1 = (sig(ga1_ref[...]) * a[:, half:] + sig(gb1_ref[...]) * b[:, half:]).astype(BF16)
    y = (jnp.dot(m0, wo_ref[:half, :], preferred_element_type=F32)
         + jnp.dot(m1, wo_ref[half:, :], preferred_element_type=F32))
    o_ref[...] = x_ref[...] + g1_ref[...] * y


def _merge(ya, yb, h, x2d, g13, mod_spec, wf, wb, wo, tm):
    rows = x2d.shape[0]
    part = lambda p: pl.BlockSpec((None, tm, W_ATT), lambda i: (p, i, 0))
    const = lambda shape: pl.BlockSpec(shape, lambda i: (0, 0), pipeline_mode=pl.Buffered(1))
    return pl.pallas_call(
        _merge_kernel,
        grid=(rows // tm,),
        in_specs=[pl.BlockSpec((tm, W_ATT), lambda i: (i, 0)), pl.BlockSpec((tm, W_ATT), lambda i: (i, 0)),
                  part(6), part(7), part(8), part(9),
                  pl.BlockSpec((tm, D_MODEL), lambda i: (i, 0)),
                  mod_spec,
                  const((W_ATT, D_MODEL)), const((W_ATT, D_MODEL)), const((D_MODEL, D_MODEL))],
        out_specs=pl.BlockSpec((tm, D_MODEL), lambda i: (i, 0)),
        out_shape=jax.ShapeDtypeStruct((rows, D_MODEL), F32),
        compiler_params=_params("arbitrary"),
        name="merge",
    )(ya, yb, h, h, h, h, x2d, g13, wf, wb, wo)


def _peer_query_kernel(x_ref, gain_ref, sc_ref, sh_ref, w_ref, q_ref, xf_ref, xm_scr):
    @pl.when(pl.program_id(1) == 0)
    def _():
        xf = _norm_mod(x_ref[...], gain_ref[...], sc_ref[...], sh_ref[...])
        xf_ref[...] = xf
        xm_scr[...] = xf.astype(BF16)

    q_ref[...] = jnp.dot(xm_scr[...], w_ref[...], preferred_element_type=F32)


def _peer_query(x2d, gain, sc3, sh3, mod_spec, wq, tm):
    rows = x2d.shape[0]
    tn = 1024
    return pl.pallas_call(
        _peer_query_kernel,
        grid=(rows // tm, wq.shape[1] // tn),
        in_specs=[pl.BlockSpec((tm, D_MODEL), lambda i, j: (i, 0)),
                  pl.BlockSpec((1, D_MODEL), lambda i, j: (0, 0)),
                  mod_spec, mod_spec,
                  pl.BlockSpec((D_MODEL, tn), lambda i, j: (0, j))],
        out_specs=[pl.BlockSpec((tm, tn), lambda i, j: (i, j)),
                   pl.BlockSpec((tm, D_MODEL), lambda i, j: (i, 0))],
        out_shape=[jax.ShapeDtypeStruct((rows, wq.shape[1]), F32),
                   jax.ShapeDtypeStruct((rows, D_MODEL), F32)],
        scratch_shapes=[pltpu.VMEM((tm, D_MODEL), BF16)],
        compiler_params=_params("arbitrary", "arbitrary"),
        name="peer_query",
    )(x2d, gain, sc3, sh3, wq)


def _top_k_rows(s, k):
    n = s.shape[0]
    iota = lax.broadcasted_iota(jnp.int32, s.shape, 0)
    vals, idxs = [], []
    for _ in range(k):
        m = jnp.max(s, axis=0, keepdims=True)
        i = jnp.min(jnp.where(s == m, iota, n), axis=0, keepdims=True)
        vals.append(m)
        idxs.append(i)
        s = jnp.where(iota == i, -jnp.inf, s)
    return jnp.concatenate(vals, axis=0), jnp.concatenate(idxs, axis=0)


def _select_rows(table, sel):
    out = jnp.zeros(sel.shape, table.dtype)
    for r in range(table.shape[0]):
        out = jnp.where(sel == r, table[r:r + 1, :], out)
    return out


def _route_kernel(q_ref, sk_ref, e_ref, g_ref):
    k = PEER_TOPK
    e_all, g_all = [], []
    for hh in range(PEER_HEADS):
        halves = []
        for c in range(2):
            col = (2 * hh + c) * PEER_HALF
            qh = q_ref[:, col:col + PEER_HALF].astype(BF16)
            halves.append(_top_k_rows(_nt(sk_ref[hh, c].astype(BF16), qh), k))
        (s1, i1), (s2, i2) = halves
        cand = jnp.concatenate([s1[a:a + 1, :] + s2 for a in range(k)], axis=0)
        top_s, top_c = _top_k_rows(cand, k)
        e_all.append(_select_rows(i1, top_c >> (k.bit_length() - 1)) * N_KEYS + _select_rows(i2, top_c & (k - 1)))
        p = jnp.exp(top_s - top_s[0:1, :])
        g_all.append(p / jnp.sum(p, axis=0, keepdims=True))
    e_ref[...] = jnp.concatenate(e_all, axis=0).T
    g_ref[...] = jnp.concatenate(g_all, axis=0).T


def _route(qp, sub_keys, tt):
    rows = qp.shape[0]
    return pl.pallas_call(
        _route_kernel,
        grid=(rows // tt,),
        in_specs=[pl.BlockSpec((tt, qp.shape[1]), lambda i: (i, 0)),
                  pl.BlockSpec(sub_keys.shape, lambda i: (0, 0, 0, 0))],
        out_specs=[pl.BlockSpec((tt, PEER_SLOTS), lambda i: (i, 0)),
                   pl.BlockSpec((tt, PEER_SLOTS), lambda i: (i, 0))],
        out_shape=[jax.ShapeDtypeStruct((rows, PEER_SLOTS), jnp.int32),
                   jax.ShapeDtypeStruct((rows, PEER_SLOTS), F32)],
        compiler_params=_params("arbitrary"),
        name="peer_route",
    )(qp, sub_keys)


PEER_TOK = 4
PEER_NBUF = 4
PEER_AHEAD = 2
UV_ROWS = 2 * ROW_TILES
PEER_GROUP = SUBLANES
PEER_NGROUP = PEER_SLOTS // PEER_GROUP


def _peer_kernel(idx_ref, g_ref, xf_ref, x1_ref, g2_ref, nf_ref, uv_ref, y_ref,
                 buf0, buf1, buf2, buf3, p_scr, wb_scr, xf3_scr, o3_scr, sem, *, tb):
    bufs = (buf0, buf1, buf2, buf3)
    n_stage = tb // PEER_TOK
    step = pl.program_id(0)

    def issue(tok, b, k, j):
        e = idx_ref[0, tok * PEER_SLOTS + j]
        pltpu.make_async_copy(uv_ref.at[e], bufs[b].at[k, j], sem.at[b, k]).start()

    def wait(b, k):
        pltpu.make_async_copy(uv_ref.at[pl.ds(0, PEER_SLOTS)], bufs[b].at[k], sem.at[b, k]).wait()

    @pl.when(step == 0)
    def _():
        def prologue(jo, c):
            for ji in range(PEER_GROUP):
                for b in range(PEER_AHEAD):
                    for k in range(PEER_TOK):
                        issue(b * PEER_TOK + k, b, k, jo * PEER_GROUP + ji)
            return c
        lax.fori_loop(0, PEER_NGROUP, prologue, 0)

    for c in range(ROW_TILES):
        xf3_scr[:, c, :] = xf_ref[:, c * LANES:(c + 1) * LANES]

    def stage(s, b):
        buf, nb = bufs[b], (b + PEER_AHEAD) % PEER_NBUF
        t0 = s * PEER_TOK
        tn = t0 + PEER_AHEAD * PEER_TOK
        for k in range(PEER_TOK):
            wait(b, k)
        xs = [(xf3_scr[t0 + k, 0:SUBLANES, :], xf3_scr[t0 + k, SUBLANES:ROW_TILES, :])
              for k in range(PEER_TOK)]

        def dot_body(jo, c):
            for ji in range(PEER_GROUP):
                j = jo * PEER_GROUP + ji
                issue(tn, nb, 0, j)
                issue(tn + 1, nb, 1, j)
                row = pl.multiple_of(j * SUBLANES, SUBLANES)
                for k in range(PEER_TOK):
                    u = buf[k, j, 0:ROW_TILES, :].astype(F32)
                    p = u[0:SUBLANES] * xs[k][0] + u[SUBLANES:ROW_TILES] * xs[k][1]
                    p_scr[k, pl.ds(row, SUBLANES), :] = p
            return c
        lax.fori_loop(0, PEER_NGROUP, dot_body, 0)

        for k in range(PEER_TOK):
            ps = p_scr[k, pl.ds(0, PEER_SLOTS, stride=SUBLANES), :]
            for r in range(1, SUBLANES):
                ps = ps + p_scr[k, pl.ds(r, PEER_SLOTS, stride=SUBLANES), :]
            h = jnp.sum(ps.T, axis=0, keepdims=True)
            w = g_ref[pl.ds(t0 + k, 1), :] * (0.5 * h * (1.0 + lax.erf(h * (2.0 ** -0.5))))
            wb_scr[k] = jnp.broadcast_to(w, (PEER_SLOTS, PEER_SLOTS)).T

        def acc_body(jo, accs):
            accs = list(accs)
            for ji in range(PEER_GROUP):
                j = jo * PEER_GROUP + ji
                issue(tn + 2, nb, 2, j)
                issue(tn + 3, nb, 3, j)
                for k in range(PEER_TOK):
                    wv = wb_scr[k, pl.ds(j, 1), :]
                    v = buf[k, j, ROW_TILES:UV_ROWS, :].astype(F32)
                    accs[2 * k] = accs[2 * k] + v[0:SUBLANES] * wv
                    accs[2 * k + 1] = accs[2 * k + 1] + v[SUBLANES:ROW_TILES] * wv
            return tuple(accs)
        zero = jnp.zeros((SUBLANES, LANES), F32)
        accs = lax.fori_loop(0, PEER_NGROUP, acc_body, (zero,) * (2 * PEER_TOK))
        for k in range(PEER_TOK):
            o3_scr[t0 + k, 0:SUBLANES, :] = accs[2 * k]
            o3_scr[t0 + k, SUBLANES:ROW_TILES, :] = accs[2 * k + 1]

    def rotation(q, c):
        for b in range(PEER_NBUF):
            stage(q * PEER_NBUF + b, b)
        return c
    lax.fori_loop(0, n_stage // PEER_NBUF, rotation, 0)

    @pl.when(step == pl.num_programs(0) - 1)
    def _():
        for b in range(PEER_AHEAD):
            for k in range(PEER_TOK):
                wait(b, k)

    ss = jnp.zeros((tb, 1), F32)
    for c in range(ROW_TILES):
        sl = slice(c * LANES, (c + 1) * LANES)
        z = x1_ref[:, sl] + g2_ref[:, sl] * o3_scr[:, c, :]
        y_ref[:, sl] = z
        ss = ss + jnp.sum(z * z, axis=1, keepdims=True)
    inv = lax.rsqrt(ss * (1.0 / D_MODEL) + EPS)
    for c in range(ROW_TILES):
        sl = slice(c * LANES, (c + 1) * LANES)
        y_ref[:, sl] = y_ref[:, sl] * inv * nf_ref[:, sl]


def _peer(idx, gate, xf, x1, g23, g2_spec, nf, uv, tb):
    rows = xf.shape[0]
    nblk = rows // tb
    tok = lambda: pl.BlockSpec((tb, D_MODEL), lambda i: (i, 0))
    n_ahead = PEER_AHEAD * PEER_TOK * PEER_SLOTS
    idx2 = idx.reshape(nblk, tb * PEER_SLOTS)
    nxt = jnp.concatenate([idx2[1:, :n_ahead], jnp.zeros((1, n_ahead), idx.dtype)], axis=0)
    idx_ext = jnp.concatenate([idx2, nxt], axis=1).reshape(nblk, 1, tb * PEER_SLOTS + n_ahead)
    gather_buf = pltpu.VMEM((PEER_TOK, PEER_SLOTS, UV_ROWS, LANES), uv.dtype)
    return pl.pallas_call(
        functools.partial(_peer_kernel, tb=tb),
        grid=(nblk,),
        in_specs=[pl.BlockSpec((None, 1, tb * PEER_SLOTS + n_ahead), lambda i: (i, 0, 0),
                               memory_space=pltpu.SMEM),
                  pl.BlockSpec((tb, PEER_SLOTS), lambda i: (i, 0)),
                  tok(), tok(), g2_spec,
                  pl.BlockSpec((1, D_MODEL), lambda i: (0, 0)),
                  pl.BlockSpec(memory_space=pl.ANY)],
        out_specs=tok(),
        out_shape=jax.ShapeDtypeStruct((rows, D_MODEL), F32),
        scratch_shapes=[gather_buf] * PEER_NBUF + [
                        pltpu.VMEM((PEER_TOK, PEER_SLOTS * SUBLANES, LANES), F32),
                        pltpu.VMEM((PEER_TOK, PEER_SLOTS, LANES), F32),
                        pltpu.VMEM((tb, ROW_TILES, LANES), F32),
                        pltpu.VMEM((tb, ROW_TILES, LANES), F32),
                        pltpu.SemaphoreType.DMA((PEER_NBUF, PEER_TOK))],
        compiler_params=_params("arbitrary"),
        name="peer_experts",
    )(idx_ext, gate, xf, x1, g23, nf, uv)


def _stream(x, mods, weights, attend, tm, tb):
    batch, seq, _ = x.shape
    rows = batch * seq
    tm = min(tm, rows)
    sh1, sc1, g1, sh2, sc2, g2 = mods
    x2d = x.reshape(rows, D_MODEL)
    if seq >= tm:
        per = seq // tm
        mod3 = lambda m: m.reshape(batch, 1, D_MODEL)
        mod_spec2 = pl.BlockSpec((None, 1, D_MODEL), lambda i, j: (i // per, 0, 0))
        mod_spec1 = pl.BlockSpec((None, 1, D_MODEL), lambda i: (i // per, 0, 0))
    else:
        mod3 = lambda m: jnp.repeat(m, seq, axis=0).reshape(rows // tm, tm, D_MODEL)
        mod_spec2 = pl.BlockSpec((None, tm, D_MODEL), lambda i, j: (i, 0, 0))
        mod_spec1 = pl.BlockSpec((None, tm, D_MODEL), lambda i: (i, 0, 0))

    h, logf_t, kv = _inproj(x2d, weights["norm_mix"], mod3(sc1), mod3(sh1), mod_spec2,
                            weights["w_cat"], weights["wff_t"], weights["bf_col"], tm)
    ya, yb = attend(h, logf_t)
    tm2 = min(tm, 256)
    if seq >= tm2:
        per2 = seq // tm2
        g13 = g1.reshape(batch, 1, D_MODEL)
        g1_spec = pl.BlockSpec((None, 1, D_MODEL), lambda i: (i // per2, 0, 0))
    else:
        g13 = jnp.repeat(g1, seq, axis=0).reshape(rows // tm2, tm2, D_MODEL)
        g1_spec = pl.BlockSpec((None, tm2, D_MODEL), lambda i: (i, 0, 0))
    x1 = _merge(ya, yb, h, x2d, g13, g1_spec, weights["wf"], weights["wb"], weights["wo"], tm2)
    qp, xf = _peer_query(x1, weights["norm_ffn"], mod3(sc2), mod3(sh2), mod_spec2, weights["wq"], tm)
    idx, gate = _route(qp, weights["sub_keys"], 256)
    per_tb = seq // tb
    g2_spec = pl.BlockSpec((None, 1, D_MODEL), lambda i: (i // per_tb, 0, 0))
    y = _peer(idx, gate, xf, x1, g2.reshape(batch, 1, D_MODEL), g2_spec, weights["nf"], weights["uv"], tb)
    return y.reshape(batch, seq, D_MODEL), h, kv, logf_t


def kernel(x_prompt, x_sample, c_prompt, c_sample, cache_fox_k, cache_fox_v, cache_fox_logf, cache_band_k, cache_band_v, w_ada, b_ada, norm_mix, norm_ffn, w_in, b_forget, rel_bias, w_branch_fox, w_branch_band, w_out, w_query, sub_keys, expert_u, expert_v, norm_final):
    bp, sp, _ = x_prompt.shape
    bs, ts, _ = x_sample.shape
    past = cache_fox_k.shape[2]
    win = cache_band_k.shape[2]
    n_exp = expert_u.shape[1]

    w = w_in[0]
    o_ff = 3 * W_ATT
    weights = {
        "w_cat": jnp.concatenate([w[:, :o_ff], w[:, o_ff + N_HEADS:]], axis=1).astype(BF16),
        "wff_t": w[:, o_ff:o_ff + N_HEADS].T,
        "bf_col": b_forget[0].reshape(N_HEADS, 1),
        "norm_mix": norm_mix[0].reshape(1, D_MODEL),
        "norm_ffn": norm_ffn[0].reshape(1, D_MODEL),
        "wf": w_branch_fox[0].astype(BF16),
        "wb": w_branch_band[0].astype(BF16),
        "wo": w_out[0].astype(BF16),
        "wq": w_query[0].astype(BF16),
        "sub_keys": sub_keys[0],
        "nf": norm_final.reshape(1, D_MODEL),
        "uv": jnp.concatenate([expert_u[0].reshape(n_exp, ROW_TILES, LANES),
                               expert_v[0].reshape(n_exp, ROW_TILES, LANES)], axis=1).astype(BF16),
    }
    rb = rel_bias[0]
    bias_own, bias_left = _band_prompt_bias(rb)
    bias_c, bias_n = _band_sample_bias(rb, win, ts)

    ada = _adaln(jnp.concatenate([c_prompt, c_sample], axis=0), w_ada[0], b_ada[0])
    mods_p = [ada[:bp, i * D_MODEL:(i + 1) * D_MODEL] for i in range(6)]
    mods_s = [ada[bp:, i * D_MODEL:(i + 1) * D_MODEL] for i in range(6)]

    def attend_prompt(h, logf_t):
        f = _cumsum_rows(logf_t.reshape(N_HEADS * bp, sp))
        ya = _fox_prompt(h, f.reshape(N_HEADS * bp, 1, sp), bp, sp)
        yb = _band_prompt(h, bias_own, bias_left, bp, sp)
        return ya, yb

    y_prompt, _, kv_p, logf_p = _stream(x_prompt, mods_p, weights, attend_prompt, tm=512, tb=64)


    def attend_sample(h, logf_t):
        lf_new = logf_t.reshape(N_HEADS, bs, ts).transpose(1, 0, 2)
        lf_all = jnp.concatenate([cache_fox_logf[0].transpose(0, 2, 1), lf_new], axis=2)
        n_pad = -(past + ts) % CUMSUM_CHUNK
        lf_all = jnp.pad(lf_all, ((0, 0), (0, 0), (0, n_pad)))
        f = _cumsum_rows(lf_all.reshape(bs * N_HEADS, -1)).reshape(bs, N_HEADS, -1)
        ya = _fox_sample(h, cache_fox_k, cache_fox_v, f[:, :, :past], f[:, :, past:past + LANES], bs, ts)
        yb = _band_sample(h, cache_band_k, cache_band_v, bias_c, bias_n, bs, ts)
        return ya, yb

    y_sample, _, kv_s, logf_s = _stream(x_sample, mods_s, weights, attend_sample, tm=512, tb=ts)

    heads = lambda part, b, t: part.reshape(1, b, t, N_HEADS, HEAD_DIM)
    logf_out = lambda lt, b, t: lt.T.reshape(1, b, t, N_HEADS)
    w_keep = min(BAND_WINDOW, sp)
    return (y_prompt, y_sample,
            heads(kv_p[0], bp, sp), heads(kv_p[1], bp, sp), logf_out(logf_p, bp, sp),
            heads(kv_p[2], bp, sp)[:, :, sp - w_keep:], heads(kv_p[3], bp, sp)[:, :, sp - w_keep:],
            heads(kv_s[0], bs, ts), heads(kv_s[1], bs, ts), logf_out(logf_s, bs, ts),
            heads(kv_s[2], bs, ts), heads(kv_s[3], bs, ts))
```

```python
import functools

import numpy as np
import jax
import jax.numpy as jnp
from jax import lax
from jax.experimental import pallas as pl
from jax.experimental.pallas import tpu as pltpu

D_MODEL = 2048
HEAD_DIM = 128
N_HEADS = 8
W_ATT = N_HEADS * HEAD_DIM
CHUNK = 64
LEFT_CHUNKS = 8
BAND_WINDOW = LEFT_CHUNKS * CHUNK
REL_MAX = 128
N_KEYS = 128
PEER_HEADS = 8
PEER_TOPK = 16
PEER_HALF = 128
PEER_SLOTS = PEER_HEADS * PEER_TOPK
EPS = 1e-6
ATT_SCALE = HEAD_DIM ** -0.5

LANES = 128
SUBLANES = 8
VMEM_LIMIT = 56 * 1024 * 1024
ROW_TILES = D_MODEL // LANES

F32 = jnp.float32
BF16 = jnp.bfloat16
NT_DIMS = (((1,), (1,)), ((), ()))


def _params(*sem):
    return pltpu.CompilerParams(dimension_semantics=sem, vmem_limit_bytes=VMEM_LIMIT)


def _nt(a, b):
    return lax.dot_general(a, b, NT_DIMS, preferred_element_type=F32)


def _norm_mod(x, gain, sc, sh):
    y = x * lax.rsqrt(jnp.mean(x * x, axis=-1, keepdims=True) + EPS)
    return (y * gain) * (1.0 + sc) + sh


def _split_bf16(x):
    hi = x.astype(BF16)
    lo = (x - hi.astype(F32)).astype(BF16)
    return hi, lo


def _adaln_kernel(c_ref, w_ref, b_ref, o_ref):
    c = c_ref[...]
    a = (c * (1.0 / (1.0 + jnp.exp(-c)))).astype(BF16)
    o_ref[...] = jnp.dot(a, w_ref[...].astype(BF16), preferred_element_type=F32) + b_ref[...]


def _adaln(c, w_ada, b_ada):
    rows, tn = c.shape[0], 1024
    n = w_ada.shape[1]
    return pl.pallas_call(
        _adaln_kernel,
        grid=(n // tn,),
        in_specs=[pl.BlockSpec((rows, D_MODEL), lambda j: (0, 0)),
                  pl.BlockSpec((D_MODEL, tn), lambda j: (0, j)),
                  pl.BlockSpec((1, tn), lambda j: (0, j))],
        out_specs=pl.BlockSpec((rows, tn), lambda j: (0, j)),
        out_shape=jax.ShapeDtypeStruct((rows, n), F32),
        compiler_params=_params("arbitrary"),
        name="adaln",
    )(c, w_ada, b_ada.reshape(1, n))


KV_PARTS = (1, 2, 4, 5)
N_ATT_PARTS = 6


def _inproj_kernel(x_ref, gain_ref, sc_ref, sh_ref, w_ref, wff_ref, bf_ref,
                   hb_ref, hg_ref, logf_ref, fk_ref, fv_ref, bk_ref, bv_ref, xm_scr):
    j = pl.program_id(1)

    @pl.when(j == 0)
    def _():
        xm = _norm_mod(x_ref[...], gain_ref[...], sc_ref[...], sh_ref[...])
        hi, lo = _split_bf16(xm)
        xm_scr[...] = hi
        whi, wlo = _split_bf16(wff_ref[...])
        z = _nt(whi, hi) + _nt(whi, lo) + _nt(wlo, hi) + bf_ref[...]
        logf_ref[...] = jnp.minimum(z, 0.0) - jnp.log1p(jnp.exp(-jnp.abs(z)))

    res = jnp.dot(xm_scr[...], w_ref[...], preferred_element_type=F32)

    @pl.when(j < N_ATT_PARTS)
    def _():
        hb_ref[...] = res.astype(BF16)

    @pl.when(j >= N_ATT_PARTS)
    def _():
        hg_ref[...] = res

    for part, kv_ref in zip(KV_PARTS, (fk_ref, fv_ref, bk_ref, bv_ref)):
        @pl.when(j == part)
        def _(kv_ref=kv_ref):
            for hh in range(N_HEADS):
                kv_ref[:, hh, :] = res[:, hh * HEAD_DIM:(hh + 1) * HEAD_DIM]


def _inproj(x2d, gain, sc3, sh3, mod_spec, w_cat, wff_t, bf_col, tm):
    rows = x2d.shape[0]
    tn = W_ATT
    nparts = w_cat.shape[1] // tn
    return pl.pallas_call(
        _inproj_kernel,
        grid=(rows // tm, nparts),
        in_specs=[pl.BlockSpec((tm, D_MODEL), lambda i, j: (i, 0)),
                  pl.BlockSpec((1, D_MODEL), lambda i, j: (0, 0)),
                  mod_spec, mod_spec,
                  pl.BlockSpec((D_MODEL, tn), lambda i, j: (0, j)),
                  pl.BlockSpec((N_HEADS, D_MODEL), lambda i, j: (0, 0)),
                  pl.BlockSpec((N_HEADS, 1), lambda i, j: (0, 0))],
        out_specs=[pl.BlockSpec((None, tm, tn), lambda i, j: (jnp.minimum(j, N_ATT_PARTS - 1), i, 0)),
                   pl.BlockSpec((None, tm, tn), lambda i, j: (jnp.maximum(j - N_ATT_PARTS, 0), i, 0)),
                   pl.BlockSpec((N_HEADS, tm), lambda i, j: (0, i)),
                   ] + [pl.BlockSpec((tm, N_HEADS, HEAD_DIM), lambda i, j: (i, 0, 0))] * len(KV_PARTS),
        out_shape=[jax.ShapeDtypeStruct((N_ATT_PARTS, rows, tn), BF16),
                   jax.ShapeDtypeStruct((nparts - N_ATT_PARTS, rows, tn), F32),
                   jax.ShapeDtypeStruct((N_HEADS, rows), F32),
                   ] + [jax.ShapeDtypeStruct((rows, N_HEADS, HEAD_DIM), F32)] * len(KV_PARTS),
        scratch_shapes=[pltpu.VMEM((tm, D_MODEL), BF16)],
        compiler_params=_params("arbitrary", "arbitrary"),
        name="inproj",
    )(x2d, gain, sc3, sh3, w_cat, wff_t, bf_col)


CUMSUM_CHUNK = 512


def _cumsum_kernel(x_ref, o_ref):
    c = CUMSUM_CHUNK
    r = lax.broadcasted_iota(jnp.int32, (c, c), 0)
    q = lax.broadcasted_iota(jnp.int32, (c, c), 1)
    tri = jnp.where(r <= q, 1.0, 0.0).astype(BF16)
    carry = jnp.zeros((SUBLANES, 1), F32)
    for k in range(x_ref.shape[1] // c):
        x = x_ref[:, k * c:(k + 1) * c]
        hi = x.astype(BF16)
        r1 = x - hi.astype(F32)
        mid = r1.astype(BF16)
        lo = (r1 - mid.astype(F32)).astype(BF16)
        y = (jnp.dot(hi, tri, preferred_element_type=F32) + jnp.dot(mid, tri, preferred_element_type=F32)
             + jnp.dot(lo, tri, preferred_element_type=F32)) + carry
        o_ref[:, k * c:(k + 1) * c] = y
        carry = y[:, c - 1:c]


def _cumsum_rows(x):
    rows, n = x.shape
    return pl.pallas_call(
        _cumsum_kernel,
        grid=(rows // SUBLANES,),
        in_specs=[pl.BlockSpec((SUBLANES, n), lambda i: (i, 0))],
        out_specs=pl.BlockSpec((SUBLANES, n), lambda i: (i, 0)),
        out_shape=jax.ShapeDtypeStruct((rows, n), F32),
        compiler_params=_params("arbitrary"),
        name="cumsum",
    )(x)


def _softmax_update(s, v, m_prev, l_prev, acc_prev):
    m_new = jnp.maximum(m_prev, jnp.max(s, axis=1, keepdims=True))
    alpha = jnp.exp(m_prev - m_new)
    p = jnp.exp(s - m_new)
    l_new = alpha * l_prev + jnp.sum(p, axis=1, keepdims=True)
    acc_new = alpha * acc_prev + jnp.dot(p.astype(BF16), v, preferred_element_type=F32)
    return m_new, l_new, acc_new


FOX_TQ = 512


def _fox_prompt_kernel(q_ref, k_ref, v_ref, f_ref, o_ref, m_scr, l_scr, acc_scr):
    tq = FOX_TQ
    qi = pl.program_id(2)
    m_scr[...] = jnp.full(m_scr.shape, -jnp.inf, F32)
    l_scr[...] = jnp.zeros(l_scr.shape, F32)
    acc_scr[...] = jnp.zeros(acc_scr.shape, F32)

    def block(ki, diagonal):
        rows = pl.ds(pl.multiple_of(ki * tq, tq), tq)
        s = _nt(q_ref[...], k_ref[rows, :]) * ATT_SCALE - f_ref[ki]
        if diagonal:
            row = lax.broadcasted_iota(jnp.int32, s.shape, 0)
            col = lax.broadcasted_iota(jnp.int32, s.shape, 1)
            s = jnp.where(col <= row, s, -jnp.inf)
        m, l, acc = _softmax_update(s, v_ref[rows, :], m_scr[...], l_scr[...], acc_scr[...])
        m_scr[...], l_scr[...], acc_scr[...] = m, l, acc

    def below_diagonal(ki, c):
        block(ki, False)
        return c
    lax.fori_loop(0, qi, below_diagonal, 0)
    block(qi, True)
    o_ref[...] = (acc_scr[...] / l_scr[...]).astype(o_ref.dtype)


def _fox_prompt(h, f, batch, seq):
    tq = FOX_TQ
    nq = seq // tq
    rows = batch * seq
    kv = lambda part: pl.BlockSpec((None, seq, HEAD_DIM), lambda b, hh, qi: (part, b, hh))
    return pl.pallas_call(
        _fox_prompt_kernel,
        grid=(batch, N_HEADS, nq),
        in_specs=[pl.BlockSpec((None, tq, HEAD_DIM), lambda b, hh, qi: (0, b * nq + qi, hh)),
                  kv(1), kv(2),
                  pl.BlockSpec((None, nq, 1, tq), lambda b, hh, qi: (hh * batch + b, 0, 0, 0))],
        out_specs=pl.BlockSpec((tq, HEAD_DIM), lambda b, hh, qi: (b * nq + qi, hh)),
        out_shape=jax.ShapeDtypeStruct((rows, W_ATT), BF16),
        scratch_shapes=[pltpu.VMEM((tq, 1), F32), pltpu.VMEM((tq, 1), F32), pltpu.VMEM((tq, HEAD_DIM), F32)],
        compiler_params=_params("arbitrary", "arbitrary", "arbitrary"),
        name="fox_prompt",
    )(h, h, h, f.reshape(N_HEADS * batch, nq, 1, tq))


BAND_TQ = BAND_WINDOW


def _band_prompt_kernel(q_ref, ko_ref, kl_ref, vo_ref, vl_ref, bo_ref, bl_ref, o_ref):
    q = q_ref[...]
    so = _nt(q, ko_ref[...]) * ATT_SCALE + bo_ref[...]
    sl = _nt(q, kl_ref[...]) * ATT_SCALE + bl_ref[...]
    sl = jnp.where(pl.program_id(2) > 0, sl, -jnp.inf)
    m = jnp.maximum(jnp.max(so, axis=1, keepdims=True), jnp.max(sl, axis=1, keepdims=True))
    po, pp = jnp.exp(so - m), jnp.exp(sl - m)
    l = jnp.sum(po, axis=1, keepdims=True) + jnp.sum(pp, axis=1, keepdims=True)
    acc = (jnp.dot(po.astype(BF16), vo_ref[...], preferred_element_type=F32)
           + jnp.dot(pp.astype(BF16), vl_ref[...], preferred_element_type=F32))
    o_ref[...] = (acc / l).astype(o_ref.dtype)


def _band_prompt(h, bias_own, bias_left, batch, seq):
    tq = BAND_TQ
    nq = seq // tq
    own = lambda part: pl.BlockSpec((None, tq, HEAD_DIM), lambda b, hh, qi: (part, b * nq + qi, hh))
    left = lambda part: pl.BlockSpec(
        (None, tq, HEAD_DIM), lambda b, hh, qi: (part, b * nq + jnp.maximum(qi - 1, 0), hh))
    bias = pl.BlockSpec((None, tq, tq), lambda b, hh, qi: (hh, 0, 0))
    return pl.pallas_call(
        _band_prompt_kernel,
        grid=(batch, N_HEADS, nq),
        in_specs=[own(3), own(4), left(4), own(5), left(5), bias, bias],
        out_specs=pl.BlockSpec((tq, HEAD_DIM), lambda b, hh, qi: (b * nq + qi, hh)),
        out_shape=jax.ShapeDtypeStruct((batch * seq, W_ATT), BF16),
        compiler_params=_params("arbitrary", "arbitrary", "arbitrary"),
        name="band_prompt",
    )(h, h, h, h, h, bias_own, bias_left)


def _skew_kernel(f_ref, o_ref):
    fb = jnp.broadcast_to(f_ref[...], o_ref.shape)
    o_ref[...] = pltpu.roll(fb, 0, 1, stride=1, stride_axis=0)


def _rel_bias_toeplitz(rel_bias, n_rows, n_cols, shift):
    hh = rel_bias.shape[0]
    w = -(-(n_rows + n_cols - 1) // LANES) * LANES
    m = np.arange(w)
    m = np.where(m < n_cols, m, m - w)
    f = rel_bias[:, np.clip(shift - m, -REL_MAX, REL_MAX) + REL_MAX].astype(F32)
    table = pl.pallas_call(
        _skew_kernel,
        grid=(hh,),
        in_specs=[pl.BlockSpec((None, 1, w), lambda h: (h, 0, 0))],
        out_specs=pl.BlockSpec((None, n_rows, w), lambda h: (h, 0, 0)),
        out_shape=jax.ShapeDtypeStruct((hh, n_rows, w), F32),
        compiler_params=_params("arbitrary"),
        name="rel_bias_table",
    )(f.reshape(hh, 1, w))
    return table[:, :, :n_cols]


def _band_prompt_bias(rel_bias):
    r = np.arange(BAND_TQ)[:, None]
    c = np.arange(BAND_TQ)[None, :]
    own = jnp.where((c // CHUNK <= r // CHUNK)[None], _rel_bias_toeplitz(rel_bias, BAND_TQ, BAND_TQ, 0), -jnp.inf)
    left = jnp.where((c // CHUNK >= r // CHUNK)[None],
                     _rel_bias_toeplitz(rel_bias, BAND_TQ, BAND_TQ, BAND_TQ), -jnp.inf)
    return own, left


def _fox_sample_kernel(q_ref, kc_hbm, vc_hbm, fc_ref, kn_ref, vn_ref, fn_ref, o_ref,
                       kbuf, vbuf, sem, m_scr, l_scr, acc_scr, *, nk, tk, t_new):
    b, ki = pl.program_id(0), pl.program_id(1)
    n = b * nk + ki
    slot = n % 2

    def cache_copies(bb, kk, sl_):
        start = pl.multiple_of(kk * tk, tk)
        cps = []
        for hh in range(N_HEADS):
            cps.append(pltpu.make_async_copy(kc_hbm.at[0, bb, pl.ds(start, tk), hh, :], kbuf.at[sl_, hh],
                                             sem.at[sl_, 0]))
            cps.append(pltpu.make_async_copy(vc_hbm.at[0, bb, pl.ds(start, tk), hh, :], vbuf.at[sl_, hh],
                                             sem.at[sl_, 1]))
        return cps

    @pl.when(n == 0)
    def _():
        for cp in cache_copies(b, ki, slot):
            cp.start()

    @pl.when(n + 1 < pl.num_programs(0) * nk)
    def _():
        last = ki == nk - 1
        for cp in cache_copies(jnp.where(last, b + 1, b), jnp.where(last, 0, ki + 1), 1 - slot):
            cp.start()

    @pl.when(ki == 0)
    def _():
        m_scr[...] = jnp.full(m_scr.shape, -jnp.inf, F32)
        l_scr[...] = jnp.zeros(l_scr.shape, F32)
        acc_scr[...] = jnp.zeros(acc_scr.shape, F32)

    for cp in cache_copies(b, ki, slot):
        cp.wait()

    for hh in range(N_HEADS):
        sl = slice(hh * HEAD_DIM, (hh + 1) * HEAD_DIM)
        s = _nt(q_ref[:, sl], kbuf[slot, hh].astype(BF16)) * ATT_SCALE - fc_ref[hh:hh + 1, :]
        m, l, acc = _softmax_update(s, vbuf[slot, hh].astype(BF16), m_scr[hh], l_scr[hh], acc_scr[hh])
        m_scr[hh], l_scr[hh], acc_scr[hh] = m, l, acc

    @pl.when(ki == nk - 1)
    def _():
        for hh in range(N_HEADS):
            sl = slice(hh * HEAD_DIM, (hh + 1) * HEAD_DIM)
            s = _nt(q_ref[:, sl], kn_ref[:, sl]) * ATT_SCALE - fn_ref[hh:hh + 1, 0:t_new]
            row = lax.broadcasted_iota(jnp.int32, s.shape, 0)
            col = lax.broadcasted_iota(jnp.int32, s.shape, 1)
            s = jnp.where(col <= row, s, -jnp.inf)
            m, l, acc = _softmax_update(s, vn_ref[:, sl], m_scr[hh], l_scr[hh], acc_scr[hh])
            o_ref[:, sl] = (acc / l).astype(o_ref.dtype)


def _fox_sample(h, cache_k, cache_v, f_cache, f_new, batch, t_new, tk=1024):
    past = cache_k.shape[2]
    nk = past // tk
    new = lambda part: pl.BlockSpec((None, t_new, W_ATT), lambda b, ki: (part, b, 0))
    cache = pl.BlockSpec(memory_space=pl.ANY)
    head_major = pltpu.VMEM((2, N_HEADS, tk, HEAD_DIM), cache_k.dtype)
    return pl.pallas_call(
        functools.partial(_fox_sample_kernel, nk=nk, tk=tk, t_new=t_new),
        grid=(batch, nk),
        in_specs=[new(0), cache, cache,
                  pl.BlockSpec((None, N_HEADS, tk), lambda b, ki: (b, 0, ki)),
                  new(1), new(2),
                  pl.BlockSpec((None, N_HEADS, LANES), lambda b, ki: (b, 0, 0))],
        out_specs=pl.BlockSpec((t_new, W_ATT), lambda b, ki: (b, 0)),
        out_shape=jax.ShapeDtypeStruct((batch * t_new, W_ATT), BF16),
        scratch_shapes=[head_major, head_major, pltpu.SemaphoreType.DMA((2, 2)),
                        pltpu.VMEM((N_HEADS, t_new, 1), F32), pltpu.VMEM((N_HEADS, t_new, 1), F32),
                        pltpu.VMEM((N_HEADS, t_new, HEAD_DIM), F32)],
        compiler_params=_params("arbitrary", "arbitrary"),
        name="fox_sample",
    )(h, cache_k, cache_v, f_cache, h, h, f_new)


def _band_sample_kernel(q_ref, kc_ref, vc_ref, kn_ref, vn_ref, bc_ref, bn_ref, o_ref):
    for hh in range(N_HEADS):
        sl = slice(hh * HEAD_DIM, (hh + 1) * HEAD_DIM)
        q = q_ref[:, sl]
        sc = _nt(q, kc_ref[:, hh, :].astype(BF16)) * ATT_SCALE + bc_ref[hh]
        sn = _nt(q, kn_ref[:, sl]) * ATT_SCALE + bn_ref[hh]
        m = jnp.maximum(jnp.max(sc, axis=1, keepdims=True), jnp.max(sn, axis=1, keepdims=True))
        pc, pn = jnp.exp(sc - m), jnp.exp(sn - m)
        l = jnp.sum(pc, axis=1, keepdims=True) + jnp.sum(pn, axis=1, keepdims=True)
        acc = (jnp.dot(pc.astype(BF16), vc_ref[:, hh, :].astype(BF16), preferred_element_type=F32)
               + jnp.dot(pn.astype(BF16), vn_ref[:, sl], preferred_element_type=F32))
        o_ref[:, sl] = (acc / l).astype(o_ref.dtype)


def _band_sample(h, cache_k, cache_v, bias_c, bias_n, batch, t_new):
    win = cache_k.shape[2]
    new = lambda part: pl.BlockSpec((None, t_new, W_ATT), lambda b: (part, b, 0))
    cache = pl.BlockSpec((None, None, win, N_HEADS, HEAD_DIM), lambda b: (0, b, 0, 0, 0))
    return pl.pallas_call(
        _band_sample_kernel,
        grid=(batch,),
        in_specs=[new(3), cache, cache, new(4), new(5),
                  pl.BlockSpec((N_HEADS, t_new, win), lambda b: (0, 0, 0)),
                  pl.BlockSpec((N_HEADS, t_new, t_new), lambda b: (0, 0, 0))],
        out_specs=pl.BlockSpec((t_new, W_ATT), lambda b: (b, 0)),
        out_shape=jax.ShapeDtypeStruct((batch * t_new, W_ATT), BF16),
        compiler_params=_params("arbitrary"),
        name="band_sample",
    )(h, cache_k, cache_v, h, h, bias_c, bias_n)


def _band_sample_bias(rel_bias, win, t_new):
    b = _rel_bias_toeplitz(rel_bias, t_new, win + t_new, win)
    return b[:, :, :win], b[:, :, win:]


def _merge_kernel(ya_ref, yb_ref, ga0_ref, ga1_ref, gb0_ref, gb1_ref, x_ref, g1_ref,
                  wf_ref, wb_ref, wo_ref, o_ref):
    a = jnp.dot(ya_ref[...], wf_ref[...], preferred_element_type=F32)
    b = jnp.dot(yb_ref[...], wb_ref[...], preferred_element_type=F32)
    sig = lambda z: 1.0 / (1.0 + jnp.exp(-z))
    half = D_MODEL // 2
    m0 = (sig(ga0_ref[...]) * a[:, :half] + sig(gb0_ref[...]) * b[:, :half]).astype(BF16)
    m1 = (sig(ga1_ref[...]) * a[:, half:] + sig(gb1_ref[...]) * b[:, half:]).astype(BF16)
    y = (jnp.dot(m0, wo_ref[:half, :], preferred_element_type=F32)
         + jnp.dot(m1, wo_ref[half:, :], preferred_element_type=F32))
    o_ref[...] = x_ref[...] + g1_ref[...] * y


def _merge(ya, yb, h, x2d, g13, mod_spec, wf, wb, wo, tm):
    rows = x2d.shape[0]
    part = lambda p: pl.BlockSpec((None, tm, W_ATT), lambda i: (p, i, 0))
    const = lambda shape: pl.BlockSpec(shape, lambda i: (0, 0), pipeline_mode=pl.Buffered(1))
    return pl.pallas_call(
        _merge_kernel,
        grid=(rows // tm,),
        in_specs=[pl.BlockSpec((tm, W_ATT), lambda i: (i, 0)), pl.BlockSpec((tm, W_ATT), lambda i: (i, 0)),
                  part(0), part(1), part(2), part(3),
                  pl.BlockSpec((tm, D_MODEL), lambda i: (i, 0)),
                  mod_spec,
                  const((W_ATT, D_MODEL)), const((W_ATT, D_MODEL)), const((D_MODEL, D_MODEL))],
        out_specs=pl.BlockSpec((tm, D_MODEL), lambda i: (i, 0)),
        out_shape=jax.ShapeDtypeStruct((rows, D_MODEL), F32),
        compiler_params=_params("arbitrary"),
        name="merge",
    )(ya, yb, h, h, h, h, x2d, g13, wf, wb, wo)


def _peer_query_kernel(x_ref, gain_ref, sc_ref, sh_ref, w_ref, q_ref, xf_ref, xm_scr):
    @pl.when(pl.program_id(1) == 0)
    def _():
        xf = _norm_mod(x_ref[...], gain_ref[...], sc_ref[...], sh_ref[...])
        xf_ref[...] = xf
        xm_scr[...] = xf.astype(BF16)

    q_ref[...] = jnp.dot(xm_scr[...], w_ref[...], preferred_element_type=F32)


def _peer_query(x2d, gain, sc3, sh3, mod_spec, wq, tm):
    rows = x2d.shape[0]
    tn = 1024
    return pl.pallas_call(
        _peer_query_kernel,
        grid=(rows // tm, wq.shape[1] // tn),
        in_specs=[pl.BlockSpec((tm, D_MODEL), lambda i, j: (i, 0)),
                  pl.BlockSpec((1, D_MODEL), lambda i, j: (0, 0)),
                  mod_spec, mod_spec,
                  pl.BlockSpec((D_MODEL, tn), lambda i, j: (0, j))],
        out_specs=[pl.BlockSpec((tm, tn), lambda i, j: (i, j)),
                   pl.BlockSpec((tm, D_MODEL), lambda i, j: (i, 0))],
        out_shape=[jax.ShapeDtypeStruct((rows, wq.shape[1]), F32),
                   jax.ShapeDtypeStruct((rows, D_MODEL), F32)],
        scratch_shapes=[pltpu.VMEM((tm, D_MODEL), BF16)],
        compiler_params=_params("arbitrary", "arbitrary"),
        name="peer_query",
    )(x2d, gain, sc3, sh3, wq)


def _top_k_rows(s, k, ids=None):
    iota = lax.broadcasted_iota(jnp.int32, s.shape, 0) if ids is None else ids
    n = jnp.iinfo(jnp.int32).max
    vals, idxs = [], []
    for _ in range(k):
        m = jnp.max(s, axis=0, keepdims=True)
        i = jnp.min(jnp.where(s == m, iota, n), axis=0, keepdims=True)
        vals.append(m)
        idxs.append(i)
        s = jnp.where(iota == i, -jnp.inf, s)
    return jnp.concatenate(vals, axis=0), jnp.concatenate(idxs, axis=0)


def _pair_candidates(s1, s2, k):
    assert k // 2 == SUBLANES
    row = lax.broadcasted_iota(jnp.int32, (SUBLANES, s1.shape[1]), 0)
    cands, ids = [], []
    for a in range(k // 2):
        nb = k // (a + 1)
        for b0 in range(0, nb, SUBLANES):
            piece = s1[a:a + 1, :] + s2[b0:b0 + SUBLANES, :]
            if nb - b0 < SUBLANES:
                piece = jnp.where(row < nb - b0, piece, -jnp.inf)
            cands.append(piece)
            ids.append(a * k + b0 + row)
    cands.append(s1[k // 2:k, :] + s2[0:1, :])
    ids.append((k // 2 + row) * k)
    return jnp.concatenate(cands, axis=0), jnp.concatenate(ids, axis=0)


def _select_rows(table, sel):
    out = jnp.zeros(sel.shape, table.dtype)
    for r in range(table.shape[0]):
        out = jnp.where(sel == r, table[r:r + 1, :], out)
    return out


def _route_kernel(q_ref, sk_ref, e_ref, g_ref):
    k = PEER_TOPK
    e_all, g_all = [], []
    for hh in range(PEER_HEADS):
        halves = []
        for c in range(2):
            col = (2 * hh + c) * PEER_HALF
            qh = q_ref[:, col:col + PEER_HALF].astype(BF16)
            halves.append(_top_k_rows(_nt(sk_ref[hh, c].astype(BF16), qh), k))
        (s1, i1), (s2, i2) = halves
        cand, cand_ids = _pair_candidates(s1, s2, k)
        top_s, top_c = _top_k_rows(cand, k, cand_ids)
        e_all.append(_select_rows(i1, top_c >> (k.bit_length() - 1)) * N_KEYS + _select_rows(i2, top_c & (k - 1)))
        p = jnp.exp(top_s - top_s[0:1, :])
        g_all.append(p / jnp.sum(p, axis=0, keepdims=True))
    e_ref[...] = jnp.concatenate(e_all, axis=0).T
    g_ref[...] = jnp.concatenate(g_all, axis=0).T


def _route(qp, sub_keys, tt):
    rows = qp.shape[0]
    return pl.pallas_call(
        _route_kernel,
        grid=(rows // tt,),
        in_specs=[pl.BlockSpec((tt, qp.shape[1]), lambda i: (i, 0)),
                  pl.BlockSpec(sub_keys.shape, lambda i: (0, 0, 0, 0))],
        out_specs=[pl.BlockSpec((tt, PEER_SLOTS), lambda i: (i, 0)),
                   pl.BlockSpec((tt, PEER_SLOTS), lambda i: (i, 0))],
        out_shape=[jax.ShapeDtypeStruct((rows, PEER_SLOTS), jnp.int32),
                   jax.ShapeDtypeStruct((rows, PEER_SLOTS), F32)],
        compiler_params=_params("arbitrary"),
        name="peer_route",
    )(qp, sub_keys)


PEER_TOK = 4
PEER_NBUF = 4
PEER_AHEAD = 2
UV_ROWS = 2 * ROW_TILES
PEER_GROUP = SUBLANES
PEER_NGROUP = PEER_SLOTS // PEER_GROUP


def _peer_kernel(idx_ref, g_ref, xf_ref, x1_ref, g2_ref, nf_ref, uv_ref, y_ref,
                 buf0, buf1, buf2, buf3, p_scr, wb_scr, xf3_scr, o3_scr, sem, *, tb):
    bufs = (buf0, buf1, buf2, buf3)
    n_stage = tb // PEER_TOK
    step = pl.program_id(0)

    def issue(tok, b, k, j):
        e = idx_ref[0, tok * PEER_SLOTS + j]
        pltpu.make_async_copy(uv_ref.at[e], bufs[b].at[k, j], sem.at[b, k]).start()

    def wait(b, k):
        pltpu.make_async_copy(uv_ref.at[pl.ds(0, PEER_SLOTS)], bufs[b].at[k], sem.at[b, k]).wait()

    @pl.when(step == 0)
    def _():
        def prologue(jo, c):
            for ji in range(PEER_GROUP):
                for b in range(PEER_AHEAD):
                    for k in range(PEER_TOK):
                        issue(b * PEER_TOK + k, b, k, jo * PEER_GROUP + ji)
            return c
        lax.fori_loop(0, PEER_NGROUP, prologue, 0)

    for c in range(ROW_TILES):
        xf3_scr[:, c, :] = xf_ref[:, c * LANES:(c + 1) * LANES]

    def stage(s, b):
        buf, nb = bufs[b], (b + PEER_AHEAD) % PEER_NBUF
        t0 = s * PEER_TOK
        tn = t0 + PEER_AHEAD * PEER_TOK
        for k in range(PEER_TOK):
            wait(b, k)
        xs = [(xf3_scr[t0 + k, 0:SUBLANES, :], xf3_scr[t0 + k, SUBLANES:ROW_TILES, :])
              for k in range(PEER_TOK)]

        def dot_body(jo, c):
            for ji in range(PEER_GROUP):
                j = jo * PEER_GROUP + ji
                issue(tn, nb, 0, j)
                issue(tn + 1, nb, 1, j)
                row = pl.multiple_of(j * SUBLANES, SUBLANES)
                for k in range(PEER_TOK):
                    u = buf[k, j, 0:ROW_TILES, :].astype(F32)
                    p = u[0:SUBLANES] * xs[k][0] + u[SUBLANES:ROW_TILES] * xs[k][1]
                    p_scr[k, pl.ds(row, SUBLANES), :] = p
            return c
        lax.fori_loop(0, PEER_NGROUP, dot_body, 0)

        for k in range(PEER_TOK):
            ps = p_scr[k, pl.ds(0, PEER_SLOTS, stride=SUBLANES), :]
            for r in range(1, SUBLANES):
                ps = ps + p_scr[k, pl.ds(r, PEER_SLOTS, stride=SUBLANES), :]
            h = jnp.sum(ps.T, axis=0, keepdims=True)
            w = g_ref[pl.ds(t0 + k, 1), :] * (0.5 * h * (1.0 + lax.erf(h * (2.0 ** -0.5))))
            wb_scr[k] = jnp.broadcast_to(w, (PEER_SLOTS, PEER_SLOTS)).T

        def acc_body(jo, accs):
            accs = list(accs)
            for ji in range(PEER_GROUP):
                j = jo * PEER_GROUP + ji
                issue(tn + 2, nb, 2, j)
                issue(tn + 3, nb, 3, j)
                for k in range(PEER_TOK):
                    wv = wb_scr[k, pl.ds(j, 1), :]
                    v = buf[k, j, ROW_TILES:UV_ROWS, :].astype(F32)
                    accs[2 * k] = accs[2 * k] + v[0:SUBLANES] * wv
                    accs[2 * k + 1] = accs[2 * k + 1] + v[SUBLANES:ROW_TILES] * wv
            return tuple(accs)
        zero = jnp.zeros((SUBLANES, LANES), F32)
        accs = lax.fori_loop(0, PEER_NGROUP, acc_body, (zero,) * (2 * PEER_TOK))
        for k in range(PEER_TOK):
            o3_scr[t0 + k, 0:SUBLANES, :] = accs[2 * k]
            o3_scr[t0 + k, SUBLANES:ROW_TILES, :] = accs[2 * k + 1]

    def rotation(q, c):
        for b in range(PEER_NBUF):
            stage(q * PEER_NBUF + b, b)
        return c
    lax.fori_loop(0, n_stage // PEER_NBUF, rotation, 0)

    @pl.when(step == pl.num_programs(0) - 1)
    def _():
        for b in range(PEER_AHEAD):
            for k in range(PEER_TOK):
                wait(b, k)

    ss = jnp.zeros((tb, 1), F32)
    for c in range(ROW_TILES):
        sl = slice(c * LANES, (c + 1) * LANES)
        z = x1_ref[:, sl] + g2_ref[:, sl] * o3_scr[:, c, :]
        y_ref[:, sl] = z
        ss = ss + jnp.sum(z * z, axis=1, keepdims=True)
    inv = lax.rsqrt(ss * (1.0 / D_MODEL) + EPS)
    for c in range(ROW_TILES):
        sl = slice(c * LANES, (c + 1) * LANES)
        y_ref[:, sl] = y_ref[:, sl] * inv * nf_ref[:, sl]


def _peer(idx, gate, xf, x1, g23, g2_spec, nf, uv, tb):
    rows = xf.shape[0]
    nblk = rows // tb
    tok = lambda: pl.BlockSpec((tb, D_MODEL), lambda i: (i, 0))
    n_ahead = PEER_AHEAD * PEER_TOK * PEER_SLOTS
    idx2 = idx.reshape(nblk, tb * PEER_SLOTS)
    nxt = jnp.concatenate([idx2[1:, :n_ahead], jnp.zeros((1, n_ahead), idx.dtype)], axis=0)
    idx_ext = jnp.concatenate([idx2, nxt], axis=1).reshape(nblk, 1, tb * PEER_SLOTS + n_ahead)
    gather_buf = pltpu.VMEM((PEER_TOK, PEER_SLOTS, UV_ROWS, LANES), uv.dtype)
    return pl.pallas_call(
        functools.partial(_peer_kernel, tb=tb),
        grid=(nblk,),
        in_specs=[pl.BlockSpec((None, 1, tb * PEER_SLOTS + n_ahead), lambda i: (i, 0, 0),
                               memory_space=pltpu.SMEM),
                  pl.BlockSpec((tb, PEER_SLOTS), lambda i: (i, 0)),
                  tok(), tok(), g2_spec,
                  pl.BlockSpec((1, D_MODEL), lambda i: (0, 0)),
                  pl.BlockSpec(memory_space=pl.ANY)],
        out_specs=tok(),
        out_shape=jax.ShapeDtypeStruct((rows, D_MODEL), F32),
        scratch_shapes=[gather_buf] * PEER_NBUF + [
                        pltpu.VMEM((PEER_TOK, PEER_SLOTS * SUBLANES, LANES), F32),
                        pltpu.VMEM((PEER_TOK, PEER_SLOTS, LANES), F32),
                        pltpu.VMEM((tb, ROW_TILES, LANES), F32),
                        pltpu.VMEM((tb, ROW_TILES, LANES), F32),
                        pltpu.SemaphoreType.DMA((PEER_NBUF, PEER_TOK))],
        compiler_params=_params("arbitrary"),
        name="peer_experts",
    )(idx_ext, gate, xf, x1, g23, nf, uv)


def _stream(x, mods, weights, attend, tm, tb):
    batch, seq, _ = x.shape
    rows = batch * seq
    tm = min(tm, rows)
    sh1, sc1, g1, sh2, sc2, g2 = mods
    x2d = x.reshape(rows, D_MODEL)
    if seq >= tm:
        per = seq // tm
        mod3 = lambda m: m.reshape(batch, 1, D_MODEL)
        mod_spec2 = pl.BlockSpec((None, 1, D_MODEL), lambda i, j: (i // per, 0, 0))
        mod_spec1 = pl.BlockSpec((None, 1, D_MODEL), lambda i: (i // per, 0, 0))
    else:
        mod3 = lambda m: jnp.repeat(m, seq, axis=0).reshape(rows // tm, tm, D_MODEL)
        mod_spec2 = pl.BlockSpec((None, tm, D_MODEL), lambda i, j: (i, 0, 0))
        mod_spec1 = pl.BlockSpec((None, tm, D_MODEL), lambda i: (i, 0, 0))

    hb, hg, logf_t, *kv = _inproj(x2d, weights["norm_mix"], mod3(sc1), mod3(sh1), mod_spec2,
                            weights["w_cat"], weights["wff_t"], weights["bf_col"], tm)
    ya, yb = attend(hb, logf_t)
    tm2 = min(tm, 256)
    if seq >= tm2:
        per2 = seq // tm2
        g13 = g1.reshape(batch, 1, D_MODEL)
        g1_spec = pl.BlockSpec((None, 1, D_MODEL), lambda i: (i // per2, 0, 0))
    else:
        g13 = jnp.repeat(g1, seq, axis=0).reshape(rows // tm2, tm2, D_MODEL)
        g1_spec = pl.BlockSpec((None, tm2, D_MODEL), lambda i: (i, 0, 0))
    x1 = _merge(ya, yb, hg, x2d, g13, g1_spec, weights["wf"], weights["wb"], weights["wo"], tm2)
    qp, xf = _peer_query(x1, weights["norm_ffn"], mod3(sc2), mod3(sh2), mod_spec2, weights["wq"], tm)
    idx, gate = _route(qp, weights["sub_keys"], 256)
    per_tb = seq // tb
    g2_spec = pl.BlockSpec((None, 1, D_MODEL), lambda i: (i // per_tb, 0, 0))
    y = _peer(idx, gate, xf, x1, g2.reshape(batch, 1, D_MODEL), g2_spec, weights["nf"], weights["uv"], tb)
    return y.reshape(batch, seq, D_MODEL), kv, logf_t


def kernel(x_prompt, x_sample, c_prompt, c_sample, cache_fox_k, cache_fox_v, cache_fox_logf, cache_band_k, cache_band_v, w_ada, b_ada, norm_mix, norm_ffn, w_in, b_forget, rel_bias, w_branch_fox, w_branch_band, w_out, w_query, sub_keys, expert_u, expert_v, norm_final):
    bp, sp, _ = x_prompt.shape
    bs, ts, _ = x_sample.shape
    past = cache_fox_k.shape[2]
    win = cache_band_k.shape[2]
    n_exp = expert_u.shape[1]

    w = w_in[0]
    o_ff = 3 * W_ATT
    weights = {
        "w_cat": jnp.concatenate([w[:, :o_ff], w[:, o_ff + N_HEADS:]], axis=1).astype(BF16),
        "wff_t": w[:, o_ff:o_ff + N_HEADS].T,
        "bf_col": b_forget[0].reshape(N_HEADS, 1),
        "norm_mix": norm_mix[0].reshape(1, D_MODEL),
        "norm_ffn": norm_ffn[0].reshape(1, D_MODEL),
        "wf": w_branch_fox[0].astype(BF16),
        "wb": w_branch_band[0].astype(BF16),
        "wo": w_out[0].astype(BF16),
        "wq": w_query[0].astype(BF16),
        "sub_keys": sub_keys[0],
        "nf": norm_final.reshape(1, D_MODEL),
        "uv": jnp.concatenate([expert_u[0].reshape(n_exp, ROW_TILES, LANES),
                               expert_v[0].reshape(n_exp, ROW_TILES, LANES)], axis=1).astype(BF16),
    }
    rb = rel_bias[0]
    bias_own, bias_left = _band_prompt_bias(rb)
    bias_c, bias_n = _band_sample_bias(rb, win, ts)

    ada = _adaln(jnp.concatenate([c_prompt, c_sample], axis=0), w_ada[0], b_ada[0])
    mods_p = [ada[:bp, i * D_MODEL:(i + 1) * D_MODEL] for i in range(6)]
    mods_s = [ada[bp:, i * D_MODEL:(i + 1) * D_MODEL] for i in range(6)]

    def attend_prompt(h, logf_t):
        f = _cumsum_rows(logf_t.reshape(N_HEADS * bp, sp))
        ya = _fox_prompt(h, f, bp, sp)
        yb = _band_prompt(h, bias_own, bias_left, bp, sp)
        return ya, yb

    y_prompt, kv_p, logf_p = _stream(x_prompt, mods_p, weights, attend_prompt, tm=512, tb=64)


    def attend_sample(h, logf_t):
        lf_new = logf_t.reshape(N_HEADS, bs, ts).transpose(1, 0, 2)
        lf_all = jnp.concatenate([cache_fox_logf[0].transpose(0, 2, 1), lf_new], axis=2)
        n_pad = -(past + ts) % CUMSUM_CHUNK
        lf_all = jnp.pad(lf_all, ((0, 0), (0, 0), (0, n_pad)))
        f = _cumsum_rows(lf_all.reshape(bs * N_HEADS, -1)).reshape(bs, N_HEADS, -1)
        ya = _fox_sample(h, cache_fox_k, cache_fox_v, f[:, :, :past], f[:, :, past:past + LANES], bs, ts)
        yb = _band_sample(h, cache_band_k, cache_band_v, bias_c, bias_n, bs, ts)
        return ya, yb

    y_sample, kv_s, logf_s = _stream(x_sample, mods_s, weights, attend_sample, tm=512, tb=ts)

    heads = lambda part, b, t: part.reshape(1, b, t, N_HEADS, HEAD_DIM)
    logf_out = lambda lt, b, t: lt.T.reshape(1, b, t, N_HEADS)
    w_keep = min(BAND_WINDOW, sp)
    return (y_prompt, y_sample,
            heads(kv_p[0], bp, sp), heads(kv_p[1], bp, sp), logf_out(logf_p, bp, sp),
            heads(kv_p[2], bp, sp)[:, :, sp - w_keep:], heads(kv_p[3], bp, sp)[:, :, sp - w_keep:],
            heads(kv_s[0], bs, ts), heads(kv_s[1], bs, ts), logf_out(logf_s, bs, ts),
            heads(kv_s[2], bs, ts), heads(kv_s[3], bs, ts))
```

```python
import functools

import numpy as np
import jax
import jax.numpy as jnp
from jax import lax
from jax.experimental import pallas as pl
from jax.experimental.pallas import tpu as pltpu

D_MODEL = 2048
HEAD_DIM = 128
N_HEADS = 8
W_ATT = N_HEADS * HEAD_DIM
CHUNK = 64
LEFT_CHUNKS = 8
BAND_WINDOW = LEFT_CHUNKS * CHUNK
REL_MAX = 128
N_KEYS = 128
PEER_HEADS = 8
PEER_TOPK = 16
PEER_HALF = 128
PEER_SLOTS = PEER_HEADS * PEER_TOPK
EPS = 1e-6
ATT_SCALE = HEAD_DIM ** -0.5

LANES = 128
SUBLANES = 8
VMEM_LIMIT = 56 * 1024 * 1024
ROW_TILES = D_MODEL // LANES

F32 = jnp.float32
BF16 = jnp.bfloat16
NT_DIMS = (((1,), (1,)), ((), ()))


def _params(*sem):
    return pltpu.CompilerParams(dimension_semantics=sem, vmem_limit_bytes=VMEM_LIMIT)


def _nt(a, b):
    return lax.dot_general(a, b, NT_DIMS, preferred_element_type=F32)


def _norm_mod(x, gain, sc, sh):
    y = x * lax.rsqrt(jnp.mean(x * x, axis=-1, keepdims=True) + EPS)
    return (y * gain) * (1.0 + sc) + sh


def _split_bf16(x):
    hi = x.astype(BF16)
    lo = (x - hi.astype(F32)).astype(BF16)
    return hi, lo


def _adaln_kernel(c_ref, w_ref, b_ref, o_ref):
    c = c_ref[...]
    a = (c * (1.0 / (1.0 + jnp.exp(-c)))).astype(BF16)
    o_ref[...] = jnp.dot(a, w_ref[...].astype(BF16), preferred_element_type=F32) + b_ref[...]


def _adaln(c, w_ada, b_ada):
    rows, tn = c.shape[0], 1024
    n = w_ada.shape[1]
    return pl.pallas_call(
        _adaln_kernel,
        grid=(n // tn,),
        in_specs=[pl.BlockSpec((rows, D_MODEL), lambda j: (0, 0)),
                  pl.BlockSpec((D_MODEL, tn), lambda j: (0, j)),
                  pl.BlockSpec((1, tn), lambda j: (0, j))],
        out_specs=pl.BlockSpec((rows, tn), lambda j: (0, j)),
        out_shape=jax.ShapeDtypeStruct((rows, n), F32),
        compiler_params=_params("arbitrary"),
        name="adaln",
    )(c, w_ada, b_ada.reshape(1, n))


KV_PARTS = (1, 2, 4, 5)
N_ATT_PARTS = 6


def _inproj_kernel(x_ref, gain_ref, sc_ref, sh_ref, w_ref, wff_ref, bf_ref,
                   hb_ref, hg_ref, logf_ref, fk_ref, fv_ref, bk_ref, bv_ref, xm_scr):
    j = pl.program_id(1)

    @pl.when(j == 0)
    def _():
        xm = _norm_mod(x_ref[...], gain_ref[...], sc_ref[...], sh_ref[...])
        hi, lo = _split_bf16(xm)
        xm_scr[...] = hi
        whi, wlo = _split_bf16(wff_ref[...])
        z = _nt(whi, hi) + _nt(whi, lo) + _nt(wlo, hi) + bf_ref[...]
        logf_ref[...] = jnp.minimum(z, 0.0) - jnp.log1p(jnp.exp(-jnp.abs(z)))

    res = jnp.dot(xm_scr[...], w_ref[...], preferred_element_type=F32)

    @pl.when(j < N_ATT_PARTS)
    def _():
        hb_ref[...] = res.astype(BF16)

    @pl.when(j >= N_ATT_PARTS)
    def _():
        hg_ref[...] = res

    for part, kv_ref in zip(KV_PARTS, (fk_ref, fv_ref, bk_ref, bv_ref)):
        @pl.when(j == part)
        def _(kv_ref=kv_ref):
            for hh in range(N_HEADS):
                kv_ref[:, hh, :] = res[:, hh * HEAD_DIM:(hh + 1) * HEAD_DIM]


def _inproj(x2d, gain, sc3, sh3, mod_spec, w_cat, wff_t, bf_col, tm):
    rows = x2d.shape[0]
    tn = W_ATT
    nparts = w_cat.shape[1] // tn
    return pl.pallas_call(
        _inproj_kernel,
        grid=(rows // tm, nparts),
        in_specs=[pl.BlockSpec((tm, D_MODEL), lambda i, j: (i, 0)),
                  pl.BlockSpec((1, D_MODEL), lambda i, j: (0, 0)),
                  mod_spec, mod_spec,
                  pl.BlockSpec((D_MODEL, tn), lambda i, j: (0, j)),
                  pl.BlockSpec((N_HEADS, D_MODEL), lambda i, j: (0, 0)),
                  pl.BlockSpec((N_HEADS, 1), lambda i, j: (0, 0))],
        out_specs=[pl.BlockSpec((None, tm, tn), lambda i, j: (jnp.minimum(j, N_ATT_PARTS - 1), i, 0)),
                   pl.BlockSpec((None, tm, tn), lambda i, j: (jnp.maximum(j - N_ATT_PARTS, 0), i, 0)),
                   pl.BlockSpec((N_HEADS, tm), lambda i, j: (0, i)),
                   ] + [pl.BlockSpec((tm, N_HEADS, HEAD_DIM), lambda i, j: (i, 0, 0))] * len(KV_PARTS),
        out_shape=[jax.ShapeDtypeStruct((N_ATT_PARTS, rows, tn), BF16),
                   jax.ShapeDtypeStruct((nparts - N_ATT_PARTS, rows, tn), F32),
                   jax.ShapeDtypeStruct((N_HEADS, rows), F32),
                   ] + [jax.ShapeDtypeStruct((rows, N_HEADS, HEAD_DIM), F32)] * len(KV_PARTS),
        scratch_shapes=[pltpu.VMEM((tm, D_MODEL), BF16)],
        compiler_params=_params("arbitrary", "arbitrary"),
        name="inproj",
    )(x2d, gain, sc3, sh3, w_cat, wff_t, bf_col)


CUMSUM_CHUNK = 512


def _cumsum_kernel(x_ref, o_ref):
    c = CUMSUM_CHUNK
    r = lax.broadcasted_iota(jnp.int32, (c, c), 0)
    q = lax.broadcasted_iota(jnp.int32, (c, c), 1)
    tri = jnp.where(r <= q, 1.0, 0.0).astype(BF16)
    carry = jnp.zeros((SUBLANES, 1), F32)
    for k in range(x_ref.shape[1] // c):
        x = x_ref[:, k * c:(k + 1) * c]
        hi = x.astype(BF16)
        r1 = x - hi.astype(F32)
        mid = r1.astype(BF16)
        lo = (r1 - mid.astype(F32)).astype(BF16)
        y = (jnp.dot(hi, tri, preferred_element_type=F32) + jnp.dot(mid, tri, preferred_element_type=F32)
             + jnp.dot(lo, tri, preferred_element_type=F32)) + carry
        o_ref[:, k * c:(k + 1) * c] = y
        carry = y[:, c - 1:c]


def _cumsum_rows(x):
    rows, n = x.shape
    return pl.pallas_call(
        _cumsum_kernel,
        grid=(rows // SUBLANES,),
        in_specs=[pl.BlockSpec((SUBLANES, n), lambda i: (i, 0))],
        out_specs=pl.BlockSpec((SUBLANES, n), lambda i: (i, 0)),
        out_shape=jax.ShapeDtypeStruct((rows, n), F32),
        compiler_params=_params("arbitrary"),
        name="cumsum",
    )(x)


def _softmax_update(s, v, m_prev, l_prev, acc_prev):
    m_new = jnp.maximum(m_prev, jnp.max(s, axis=1, keepdims=True))
    alpha = jnp.exp(m_prev - m_new)
    p = jnp.exp(s - m_new)
    l_new = alpha * l_prev + jnp.sum(p, axis=1, keepdims=True)
    acc_new = alpha * acc_prev + jnp.dot(p.astype(BF16), v, preferred_element_type=F32)
    return m_new, l_new, acc_new


FOX_TQ = 512
FOX_RQ = 128


def _fox_prompt_kernel(q_ref, k_ref, v_ref, f_ref, o_ref, m_scr, l_scr, acc_scr):
    tq = FOX_TQ
    qi = pl.program_id(2)
    m_scr[...] = jnp.full(m_scr.shape, -jnp.inf, F32)
    l_scr[...] = jnp.zeros(l_scr.shape, F32)
    acc_scr[...] = jnp.zeros(acc_scr.shape, F32)

    def block(ki, diagonal):
        keys = pl.ds(pl.multiple_of(ki * tq, tq), tq)
        k, v, f = k_ref[keys, :], v_ref[keys, :], f_ref[ki]
        subs = [slice(r * FOX_RQ, (r + 1) * FOX_RQ) for r in range(tq // FOX_RQ)]
        prev = [(m_scr[rs, :], l_scr[rs, :], acc_scr[rs, :]) for rs in subs]
        qk = [_nt(q_ref[rs, :], k) for rs in subs]
        new = []
        for r, rs in enumerate(subs):
            s = qk[r] * ATT_SCALE - f
            if diagonal:
                row = lax.broadcasted_iota(jnp.int32, s.shape, 0) + r * FOX_RQ
                col = lax.broadcasted_iota(jnp.int32, s.shape, 1)
                s = jnp.where(col <= row, s, -jnp.inf)
            new.append(_softmax_update(s, v, *prev[r]))
        for rs, (m, l, acc) in zip(subs, new):
            m_scr[rs, :], l_scr[rs, :], acc_scr[rs, :] = m, l, acc

    def below_diagonal(ki, c):
        block(ki, False)
        return c
    lax.fori_loop(0, qi, below_diagonal, 0)
    block(qi, True)
    o_ref[...] = (acc_scr[...] / l_scr[...]).astype(o_ref.dtype)


def _fox_prompt(h, f, batch, seq):
    tq = FOX_TQ
    nq = seq // tq
    rows = batch * seq
    kv = lambda part: pl.BlockSpec((None, seq, HEAD_DIM), lambda b, hh, qi: (part, b, hh))
    return pl.pallas_call(
        _fox_prompt_kernel,
        grid=(batch, N_HEADS, nq),
        in_specs=[pl.BlockSpec((None, tq, HEAD_DIM), lambda b, hh, qi: (0, b * nq + qi, hh)),
                  kv(1), kv(2),
                  pl.BlockSpec((None, nq, 1, tq), lambda b, hh, qi: (hh * batch + b, 0, 0, 0))],
        out_specs=pl.BlockSpec((tq, HEAD_DIM), lambda b, hh, qi: (b * nq + qi, hh)),
        out_shape=jax.ShapeDtypeStruct((rows, W_ATT), BF16),
        scratch_shapes=[pltpu.VMEM((tq, 1), F32), pltpu.VMEM((tq, 1), F32), pltpu.VMEM((tq, HEAD_DIM), F32)],
        compiler_params=_params("arbitrary", "arbitrary", "arbitrary"),
        name="fox_prompt",
    )(h, h, h, f.reshape(N_HEADS * batch, nq, 1, tq))


BAND_TQ = BAND_WINDOW


def _band_prompt_kernel(q_ref, ko_ref, kl_ref, vo_ref, vl_ref, bo_ref, bl_ref, o_ref):
    ko, kl, vo, vl = ko_ref[...], kl_ref[...], vo_ref[...], vl_ref[...]
    has_left = pl.program_id(2) > 0
    subs = [slice(r * FOX_RQ, (r + 1) * FOX_RQ) for r in range(BAND_TQ // FOX_RQ)]
    qk = [(_nt(q_ref[rs, :], ko), _nt(q_ref[rs, :], kl)) for rs in subs]
    for rs, (qo, ql) in zip(subs, qk):
        so = qo * ATT_SCALE + bo_ref[rs, :]
        sl = jnp.where(has_left, ql * ATT_SCALE + bl_ref[rs, :], -jnp.inf)
        m = jnp.maximum(jnp.max(so, axis=1, keepdims=True), jnp.max(sl, axis=1, keepdims=True))
        po, pp = jnp.exp(so - m), jnp.exp(sl - m)
        l = jnp.sum(po, axis=1, keepdims=True) + jnp.sum(pp, axis=1, keepdims=True)
        acc = (jnp.dot(po.astype(BF16), vo, preferred_element_type=F32)
               + jnp.dot(pp.astype(BF16), vl, preferred_element_type=F32))
        o_ref[rs, :] = (acc / l).astype(o_ref.dtype)


def _band_prompt(h, bias_own, bias_left, batch, seq):
    tq = BAND_TQ
    nq = seq // tq
    own = lambda part: pl.BlockSpec((None, tq, HEAD_DIM), lambda b, hh, qi: (part, b * nq + qi, hh))
    left = lambda part: pl.BlockSpec(
        (None, tq, HEAD_DIM), lambda b, hh, qi: (part, b * nq + jnp.maximum(qi - 1, 0), hh))
    bias = pl.BlockSpec((None, tq, tq), lambda b, hh, qi: (hh, 0, 0))
    return pl.pallas_call(
        _band_prompt_kernel,
        grid=(batch, N_HEADS, nq),
        in_specs=[own(3), own(4), left(4), own(5), left(5), bias, bias],
        out_specs=pl.BlockSpec((tq, HEAD_DIM), lambda b, hh, qi: (b * nq + qi, hh)),
        out_shape=jax.ShapeDtypeStruct((batch * seq, W_ATT), BF16),
        compiler_params=_params("arbitrary", "arbitrary", "arbitrary"),
        name="band_prompt",
    )(h, h, h, h, h, bias_own, bias_left)


def _skew_kernel(f_ref, o_ref):
    fb = jnp.broadcast_to(f_ref[...], o_ref.shape)
    o_ref[...] = pltpu.roll(fb, 0, 1, stride=1, stride_axis=0)


def _rel_bias_toeplitz(rel_bias, n_rows, n_cols, shift):
    hh = rel_bias.shape[0]
    w = -(-(n_rows + n_cols - 1) // LANES) * LANES
    m = np.arange(w)
    m = np.where(m < n_cols, m, m - w)
    f = rel_bias[:, np.clip(shift - m, -REL_MAX, REL_MAX) + REL_MAX].astype(F32)
    table = pl.pallas_call(
        _skew_kernel,
        grid=(hh,),
        in_specs=[pl.BlockSpec((None, 1, w), lambda h: (h, 0, 0))],
        out_specs=pl.BlockSpec((None, n_rows, w), lambda h: (h, 0, 0)),
        out_shape=jax.ShapeDtypeStruct((hh, n_rows, w), F32),
        compiler_params=_params("arbitrary"),
        name="rel_bias_table",
    )(f.reshape(hh, 1, w))
    return table[:, :, :n_cols]


def _band_prompt_bias(rel_bias):
    r = np.arange(BAND_TQ)[:, None]
    c = np.arange(BAND_TQ)[None, :]
    own = jnp.where((c // CHUNK <= r // CHUNK)[None], _rel_bias_toeplitz(rel_bias, BAND_TQ, BAND_TQ, 0), -jnp.inf)
    left = jnp.where((c // CHUNK >= r // CHUNK)[None],
                     _rel_bias_toeplitz(rel_bias, BAND_TQ, BAND_TQ, BAND_TQ), -jnp.inf)
    return own, left


def _fox_sample_kernel(q_ref, kc_hbm, vc_hbm, fc_ref, kn_ref, vn_ref, fn_ref, o_ref,
                       kbuf, vbuf, sem, m_scr, l_scr, acc_scr, *, nk, tk, t_new):
    b, ki = pl.program_id(0), pl.program_id(1)
    n = b * nk + ki
    slot = n % 2

    def cache_copies(bb, kk, sl_):
        start = pl.multiple_of(kk * tk, tk)
        cps = []
        for hh in range(N_HEADS):
            cps.append(pltpu.make_async_copy(kc_hbm.at[0, bb, pl.ds(start, tk), hh, :], kbuf.at[sl_, hh],
                                             sem.at[sl_, 0]))
            cps.append(pltpu.make_async_copy(vc_hbm.at[0, bb, pl.ds(start, tk), hh, :], vbuf.at[sl_, hh],
                                             sem.at[sl_, 1]))
        return cps

    @pl.when(n == 0)
    def _():
        for cp in cache_copies(b, ki, slot):
            cp.start()

    @pl.when(n + 1 < pl.num_programs(0) * nk)
    def _():
        last = ki == nk - 1
        for cp in cache_copies(jnp.where(last, b + 1, b), jnp.where(last, 0, ki + 1), 1 - slot):
            cp.start()

    @pl.when(ki == 0)
    def _():
        m_scr[...] = jnp.full(m_scr.shape, -jnp.inf, F32)
        l_scr[...] = jnp.zeros(l_scr.shape, F32)
        acc_scr[...] = jnp.zeros(acc_scr.shape, F32)

    for cp in cache_copies(b, ki, slot):
        cp.wait()

    for hh in range(N_HEADS):
        sl = slice(hh * HEAD_DIM, (hh + 1) * HEAD_DIM)
        s = _nt(q_ref[:, sl], kbuf[slot, hh].astype(BF16)) * ATT_SCALE - fc_ref[hh:hh + 1, :]
        m, l, acc = _softmax_update(s, vbuf[slot, hh].astype(BF16), m_scr[hh], l_scr[hh], acc_scr[hh])
        m_scr[hh], l_scr[hh], acc_scr[hh] = m, l, acc

    @pl.when(ki == nk - 1)
    def _():
        for hh in range(N_HEADS):
            sl = slice(hh * HEAD_DIM, (hh + 1) * HEAD_DIM)
            s = _nt(q_ref[:, sl], kn_ref[:, sl]) * ATT_SCALE - fn_ref[hh:hh + 1, 0:t_new]
            row = lax.broadcasted_iota(jnp.int32, s.shape, 0)
            col = lax.broadcasted_iota(jnp.int32, s.shape, 1)
            s = jnp.where(col <= row, s, -jnp.inf)
            m, l, acc = _softmax_update(s, vn_ref[:, sl], m_scr[hh], l_scr[hh], acc_scr[hh])
            o_ref[:, sl] = (acc / l).astype(o_ref.dtype)


def _fox_sample(h, cache_k, cache_v, f_cache, f_new, batch, t_new, tk=1024):
    past = cache_k.shape[2]
    nk = past // tk
    new = lambda part: pl.BlockSpec((None, t_new, W_ATT), lambda b, ki: (part, b, 0))
    cache = pl.BlockSpec(memory_space=pl.ANY)
    head_major = pltpu.VMEM((2, N_HEADS, tk, HEAD_DIM), cache_k.dtype)
    return pl.pallas_call(
        functools.partial(_fox_sample_kernel, nk=nk, tk=tk, t_new=t_new),
        grid=(batch, nk),
        in_specs=[new(0), cache, cache,
                  pl.BlockSpec((None, N_HEADS, tk), lambda b, ki: (b, 0, ki)),
                  new(1), new(2),
                  pl.BlockSpec((None, N_HEADS, LANES), lambda b, ki: (b, 0, 0))],
        out_specs=pl.BlockSpec((t_new, W_ATT), lambda b, ki: (b, 0)),
        out_shape=jax.ShapeDtypeStruct((batch * t_new, W_ATT), BF16),
        scratch_shapes=[head_major, head_major, pltpu.SemaphoreType.DMA((2, 2)),
                        pltpu.VMEM((N_HEADS, t_new, 1), F32), pltpu.VMEM((N_HEADS, t_new, 1), F32),
                        pltpu.VMEM((N_HEADS, t_new, HEAD_DIM), F32)],
        compiler_params=_params("arbitrary", "arbitrary"),
        name="fox_sample",
    )(h, cache_k, cache_v, f_cache, h, h, f_new)


def _band_sample_kernel(q_ref, kc_ref, vc_ref, kn_ref, vn_ref, bc_ref, bn_ref, o_ref):
    for hh in range(N_HEADS):
        sl = slice(hh * HEAD_DIM, (hh + 1) * HEAD_DIM)
        q = q_ref[:, sl]
        sc = _nt(q, kc_ref[:, hh, :].astype(BF16)) * ATT_SCALE + bc_ref[hh]
        sn = _nt(q, kn_ref[:, sl]) * ATT_SCALE + bn_ref[hh]
        m = jnp.maximum(jnp.max(sc, axis=1, keepdims=True), jnp.max(sn, axis=1, keepdims=True))
        pc, pn = jnp.exp(sc - m), jnp.exp(sn - m)
        l = jnp.sum(pc, axis=1, keepdims=True) + jnp.sum(pn, axis=1, keepdims=True)
        acc = (jnp.dot(pc.astype(BF16), vc_ref[:, hh, :].astype(BF16), preferred_element_type=F32)
               + jnp.dot(pn.astype(BF16), vn_ref[:, sl], preferred_element_type=F32))
        o_ref[:, sl] = (acc / l).astype(o_ref.dtype)


def _band_sample(h, cache_k, cache_v, bias_c, bias_n, batch, t_new):
    win = cache_k.shape[2]
    new = lambda part: pl.BlockSpec((None, t_new, W_ATT), lambda b: (part, b, 0))
    cache = pl.BlockSpec((None, None, win, N_HEADS, HEAD_DIM), lambda b: (0, b, 0, 0, 0))
    return pl.pallas_call(
        _band_sample_kernel,
        grid=(batch,),
        in_specs=[new(3), cache, cache, new(4), new(5),
                  pl.BlockSpec((N_HEADS, t_new, win), lambda b: (0, 0, 0)),
                  pl.BlockSpec((N_HEADS, t_new, t_new), lambda b: (0, 0, 0))],
        out_specs=pl.BlockSpec((t_new, W_ATT), lambda b: (b, 0)),
        out_shape=jax.ShapeDtypeStruct((batch * t_new, W_ATT), BF16),
        compiler_params=_params("arbitrary"),
        name="band_sample",
    )(h, cache_k, cache_v, h, h, bias_c, bias_n)


def _band_sample_bias(rel_bias, win, t_new):
    b = _rel_bias_toeplitz(rel_bias, t_new, win + t_new, win)
    return b[:, :, :win], b[:, :, win:]


def _merge_kernel(ya_ref, yb_ref, ga0_ref, ga1_ref, gb0_ref, gb1_ref, x_ref, g1_ref,
                  wf_ref, wb_ref, wo_ref, o_ref):
    a = jnp.dot(ya_ref[...], wf_ref[...], preferred_element_type=F32)
    b = jnp.dot(yb_ref[...], wb_ref[...], preferred_element_type=F32)
    sig = lambda z: 1.0 / (1.0 + jnp.exp(-z))
    half = D_MODEL // 2
    m0 = (sig(ga0_ref[...]) * a[:, :half] + sig(gb0_ref[...]) * b[:, :half]).astype(BF16)
    m1 = (sig(ga1_ref[...]) * a[:, half:] + sig(gb1_ref[...]) * b[:, half:]).astype(BF16)
    y = (jnp.dot(m0, wo_ref[:half, :], preferred_element_type=F32)
         + jnp.dot(m1, wo_ref[half:, :], preferred_element_type=F32))
    o_ref[...] = x_ref[...] + g1_ref[...] * y


def _merge(ya, yb, h, x2d, g13, mod_spec, wf, wb, wo, tm):
    rows = x2d.shape[0]
    part = lambda p: pl.BlockSpec((None, tm, W_ATT), lambda i: (p, i, 0))
    const = lambda shape: pl.BlockSpec(shape, lambda i: (0, 0), pipeline_mode=pl.Buffered(1))
    return pl.pallas_call(
        _merge_kernel,
        grid=(rows // tm,),
        in_specs=[pl.BlockSpec((tm, W_ATT), lambda i: (i, 0)), pl.BlockSpec((tm, W_ATT), lambda i: (i, 0)),
                  part(0), part(1), part(2), part(3),
                  pl.BlockSpec((tm, D_MODEL), lambda i: (i, 0)),
                  mod_spec,
                  const((W_ATT, D_MODEL)), const((W_ATT, D_MODEL)), const((D_MODEL, D_MODEL))],
        out_specs=pl.BlockSpec((tm, D_MODEL), lambda i: (i, 0)),
        out_shape=jax.ShapeDtypeStruct((rows, D_MODEL), F32),
        compiler_params=_params("arbitrary"),
        name="merge",
    )(ya, yb, h, h, h, h, x2d, g13, wf, wb, wo)


def _peer_query_kernel(x_ref, gain_ref, sc_ref, sh_ref, w_ref, q_ref, xf_ref, xm_scr):
    @pl.when(pl.program_id(1) == 0)
    def _():
        xf = _norm_mod(x_ref[...], gain_ref[...], sc_ref[...], sh_ref[...])
        xf_ref[...] = xf
        xm_scr[...] = xf.astype(BF16)

    q_ref[...] = jnp.dot(xm_scr[...], w_ref[...], preferred_element_type=F32)


def _peer_query(x2d, gain, sc3, sh3, mod_spec, wq, tm):
    rows = x2d.shape[0]
    tn = wq.shape[1]
    return pl.pallas_call(
        _peer_query_kernel,
        grid=(rows // tm, wq.shape[1] // tn),
        in_specs=[pl.BlockSpec((tm, D_MODEL), lambda i, j: (i, 0)),
                  pl.BlockSpec((1, D_MODEL), lambda i, j: (0, 0)),
                  mod_spec, mod_spec,
                  pl.BlockSpec((D_MODEL, tn), lambda i, j: (0, j), pipeline_mode=pl.Buffered(1))],
        out_specs=[pl.BlockSpec((tm, tn), lambda i, j: (i, j)),
                   pl.BlockSpec((tm, D_MODEL), lambda i, j: (i, 0))],
        out_shape=[jax.ShapeDtypeStruct((rows, wq.shape[1]), F32),
                   jax.ShapeDtypeStruct((rows, D_MODEL), F32)],
        scratch_shapes=[pltpu.VMEM((tm, D_MODEL), BF16)],
        compiler_params=_params("arbitrary", "arbitrary"),
        name="peer_query",
    )(x2d, gain, sc3, sh3, wq)


def _top_k_rows(s, k, ids=None):
    iota = lax.broadcasted_iota(jnp.int32, s.shape, 0) if ids is None else ids
    n = jnp.iinfo(jnp.int32).max
    vals, idxs = [], []
    for _ in range(k):
        m = jnp.max(s, axis=0, keepdims=True)
        i = jnp.min(jnp.where(s == m, iota, n), axis=0, keepdims=True)
        vals.append(m)
        idxs.append(i)
        s = jnp.where(iota == i, -jnp.inf, s)
    return jnp.concatenate(vals, axis=0), jnp.concatenate(idxs, axis=0)


def _pair_candidates(s1, s2, k):
    assert k // 2 == SUBLANES
    row = lax.broadcasted_iota(jnp.int32, (SUBLANES, s1.shape[1]), 0)
    cands, ids = [], []
    for a in range(k // 2):
        nb = k // (a + 1)
        for b0 in range(0, nb, SUBLANES):
            piece = s1[a:a + 1, :] + s2[b0:b0 + SUBLANES, :]
            if nb - b0 < SUBLANES:
                piece = jnp.where(row < nb - b0, piece, -jnp.inf)
            cands.append(piece)
            ids.append(a * k + b0 + row)
    cands.append(s1[k // 2:k, :] + s2[0:1, :])
    ids.append((k // 2 + row) * k)
    return jnp.concatenate(cands, axis=0), jnp.concatenate(ids, axis=0)


def _select_rows(table, sel):
    out = jnp.zeros(sel.shape, table.dtype)
    for r in range(table.shape[0]):
        out = jnp.where(sel == r, table[r:r + 1, :], out)
    return out


def _route_kernel(q_ref, sk_ref, e_ref, g_ref):
    k = PEER_TOPK
    e_all, g_all = [], []
    for hh in range(PEER_HEADS):
        halves = []
        for c in range(2):
            col = (2 * hh + c) * PEER_HALF
            qh = q_ref[:, col:col + PEER_HALF].astype(BF16)
            halves.append(_top_k_rows(_nt(sk_ref[hh, c].astype(BF16), qh), k))
        (s1, i1), (s2, i2) = halves
        cand, cand_ids = _pair_candidates(s1, s2, k)
        top_s, top_c = _top_k_rows(cand, k, cand_ids)
        e_all.append(_select_rows(i1, top_c >> (k.bit_length() - 1)) * N_KEYS + _select_rows(i2, top_c & (k - 1)))
        p = jnp.exp(top_s - top_s[0:1, :])
        g_all.append(p / jnp.sum(p, axis=0, keepdims=True))
    e_ref[...] = jnp.concatenate(e_all, axis=0).T
    g_ref[...] = jnp.concatenate(g_all, axis=0).T


def _route(qp, sub_keys, tt):
    rows = qp.shape[0]
    return pl.pallas_call(
        _route_kernel,
        grid=(rows // tt,),
        in_specs=[pl.BlockSpec((tt, qp.shape[1]), lambda i: (i, 0)),
                  pl.BlockSpec(sub_keys.shape, lambda i: (0, 0, 0, 0))],
        out_specs=[pl.BlockSpec((tt, PEER_SLOTS), lambda i: (i, 0)),
                   pl.BlockSpec((tt, PEER_SLOTS), lambda i: (i, 0))],
        out_shape=[jax.ShapeDtypeStruct((rows, PEER_SLOTS), jnp.int32),
                   jax.ShapeDtypeStruct((rows, PEER_SLOTS), F32)],
        compiler_params=_params("arbitrary"),
        name="peer_route",
    )(qp, sub_keys)


PEER_TOK = 4
PEER_NBUF = 4
PEER_AHEAD = 2
UV_ROWS = 2 * ROW_TILES
PEER_GROUP = SUBLANES
PEER_NGROUP = PEER_SLOTS // PEER_GROUP


def _peer_kernel(idx_ref, g_ref, xf_ref, x1_ref, g2_ref, nf_ref, uv_ref, y_ref,
                 buf0, buf1, buf2, buf3, p_scr, wb_scr, xf3_scr, o3_scr, sem, *, tb):
    bufs = (buf0, buf1, buf2, buf3)
    n_stage = tb // PEER_TOK
    step = pl.program_id(0)

    def issue(tok, b, k, j):
        e = idx_ref[0, tok * PEER_SLOTS + j]
        pltpu.make_async_copy(uv_ref.at[e], bufs[b].at[k, j], sem.at[b, k]).start()

    def wait(b, k):
        pltpu.make_async_copy(uv_ref.at[pl.ds(0, PEER_SLOTS)], bufs[b].at[k], sem.at[b, k]).wait()

    @pl.when(step == 0)
    def _():
        def prologue(jo, c):
            for ji in range(PEER_GROUP):
                for b in range(PEER_AHEAD):
                    for k in range(PEER_TOK):
                        issue(b * PEER_TOK + k, b, k, jo * PEER_GROUP + ji)
            return c
        lax.fori_loop(0, PEER_NGROUP, prologue, 0)

    for c in range(ROW_TILES):
        xf3_scr[:, c, :] = xf_ref[:, c * LANES:(c + 1) * LANES]

    def stage(s, b):
        buf, nb = bufs[b], (b + PEER_AHEAD) % PEER_NBUF
        t0 = s * PEER_TOK
        tn = t0 + PEER_AHEAD * PEER_TOK
        for k in range(PEER_TOK):
            wait(b, k)
        xs = [(xf3_scr[t0 + k, 0:SUBLANES, :], xf3_scr[t0 + k, SUBLANES:ROW_TILES, :])
              for k in range(PEER_TOK)]

        def dot_body(jo, c):
            for ji in range(PEER_GROUP):
                j = jo * PEER_GROUP + ji
                issue(tn, nb, 0, j)
                issue(tn + 1, nb, 1, j)
                row = pl.multiple_of(j * SUBLANES, SUBLANES)
                for k in range(PEER_TOK):
                    u = buf[k, j, 0:ROW_TILES, :].astype(F32)
                    p = u[0:SUBLANES] * xs[k][0] + u[SUBLANES:ROW_TILES] * xs[k][1]
                    p_scr[k, pl.ds(row, SUBLANES), :] = p
            return c
        lax.fori_loop(0, PEER_NGROUP, dot_body, 0)

        for k in range(PEER_TOK):
            ps = p_scr[k, pl.ds(0, PEER_SLOTS, stride=SUBLANES), :]
            for r in range(1, SUBLANES):
                ps = ps + p_scr[k, pl.ds(r, PEER_SLOTS, stride=SUBLANES), :]
            h = jnp.sum(ps.T, axis=0, keepdims=True)
            w = g_ref[pl.ds(t0 + k, 1), :] * (0.5 * h * (1.0 + lax.erf(h * (2.0 ** -0.5))))
            wb_scr[k] = jnp.broadcast_to(w, (PEER_SLOTS, PEER_SLOTS)).T

        def acc_body(jo, accs):
            accs = list(accs)
            for ji in range(PEER_GROUP):
                j = jo * PEER_GROUP + ji
                issue(tn + 2, nb, 2, j)
                issue(tn + 3, nb, 3, j)
                for k in range(PEER_TOK):
                    wv = wb_scr[k, pl.ds(j, 1), :]
                    v = buf[k, j, ROW_TILES:UV_ROWS, :].astype(F32)
                    accs[2 * k] = accs[2 * k] + v[0:SUBLANES] * wv
                    accs[2 * k + 1] = accs[2 * k + 1] + v[SUBLANES:ROW_TILES] * wv
            return tuple(accs)
        zero = jnp.zeros((SUBLANES, LANES), F32)
        accs = lax.fori_loop(0, PEER_NGROUP, acc_body, (zero,) * (2 * PEER_TOK))
        for k in range(PEER_TOK):
            o3_scr[t0 + k, 0:SUBLANES, :] = accs[2 * k]
            o3_scr[t0 + k, SUBLANES:ROW_TILES, :] = accs[2 * k + 1]

    def rotation(q, c):
        for b in range(PEER_NBUF):
            stage(q * PEER_NBUF + b, b)
        return c
    lax.fori_loop(0, n_stage // PEER_NBUF, rotation, 0)

    @pl.when(step == pl.num_programs(0) - 1)
    def _():
        for b in range(PEER_AHEAD):
            for k in range(PEER_TOK):
                wait(b, k)

    ss = jnp.zeros((tb, 1), F32)
    for c in range(ROW_TILES):
        sl = slice(c * LANES, (c + 1) * LANES)
        z = x1_ref[:, sl] + g2_ref[:, sl] * o3_scr[:, c, :]
        y_ref[:, sl] = z
        ss = ss + jnp.sum(z * z, axis=1, keepdims=True)
    inv = lax.rsqrt(ss * (1.0 / D_MODEL) + EPS)
    for c in range(ROW_TILES):
        sl = slice(c * LANES, (c + 1) * LANES)
        y_ref[:, sl] = y_ref[:, sl] * inv * nf_ref[:, sl]


def _peer(idx, gate, xf, x1, g23, g2_spec, nf, uv, tb):
    rows = xf.shape[0]
    nblk = rows // tb
    tok = lambda: pl.BlockSpec((tb, D_MODEL), lambda i: (i, 0))
    n_ahead = PEER_AHEAD * PEER_TOK * PEER_SLOTS
    idx2 = idx.reshape(nblk, tb * PEER_SLOTS)
    nxt = jnp.concatenate([idx2[1:, :n_ahead], jnp.zeros((1, n_ahead), idx.dtype)], axis=0)
    idx_ext = jnp.concatenate([idx2, nxt], axis=1).reshape(nblk, 1, tb * PEER_SLOTS + n_ahead)
    gather_buf = pltpu.VMEM((PEER_TOK, PEER_SLOTS, UV_ROWS, LANES), uv.dtype)
    return pl.pallas_call(
        functools.partial(_peer_kernel, tb=tb),
        grid=(nblk,),
        in_specs=[pl.BlockSpec((None, 1, tb * PEER_SLOTS + n_ahead), lambda i: (i, 0, 0),
                               memory_space=pltpu.SMEM),
                  pl.BlockSpec((tb, PEER_SLOTS), lambda i: (i, 0)),
                  tok(), tok(), g2_spec,
                  pl.BlockSpec((1, D_MODEL), lambda i: (0, 0)),
                  pl.BlockSpec(memory_space=pl.ANY)],
        out_specs=tok(),
        out_shape=jax.ShapeDtypeStruct((rows, D_MODEL), F32),
        scratch_shapes=[gather_buf] * PEER_NBUF + [
                        pltpu.VMEM((PEER_TOK, PEER_SLOTS * SUBLANES, LANES), F32),
                        pltpu.VMEM((PEER_TOK, PEER_SLOTS, LANES), F32),
                        pltpu.VMEM((tb, ROW_TILES, LANES), F32),
                        pltpu.VMEM((tb, ROW_TILES, LANES), F32),
                        pltpu.SemaphoreType.DMA((PEER_NBUF, PEER_TOK))],
        compiler_params=_params("arbitrary"),
        name="peer_experts",
    )(idx_ext, gate, xf, x1, g23, nf, uv)


def _stream(x, mods, weights, attend, tm, tb):
    batch, seq, _ = x.shape
    rows = batch * seq
    tm = min(tm, rows)
    sh1, sc1, g1, sh2, sc2, g2 = mods
    x2d = x.reshape(rows, D_MODEL)
    if seq >= tm:
        per = seq // tm
        mod3 = lambda m: m.reshape(batch, 1, D_MODEL)
        mod_spec2 = pl.BlockSpec((None, 1, D_MODEL), lambda i, j: (i // per, 0, 0))
        mod_spec1 = pl.BlockSpec((None, 1, D_MODEL), lambda i: (i // per, 0, 0))
    else:
        mod3 = lambda m: jnp.repeat(m, seq, axis=0).reshape(rows // tm, tm, D_MODEL)
        mod_spec2 = pl.BlockSpec((None, tm, D_MODEL), lambda i, j: (i, 0, 0))
        mod_spec1 = pl.BlockSpec((None, tm, D_MODEL), lambda i: (i, 0, 0))

    hb, hg, logf_t, *kv = _inproj(x2d, weights["norm_mix"], mod3(sc1), mod3(sh1), mod_spec2,
                            weights["w_cat"], weights["wff_t"], weights["bf_col"], tm)
    ya, yb = attend(hb, logf_t)
    tm2 = min(tm, 256)
    if seq >= tm2:
        per2 = seq // tm2
        g13 = g1.reshape(batch, 1, D_MODEL)
        g1_spec = pl.BlockSpec((None, 1, D_MODEL), lambda i: (i // per2, 0, 0))
    else:
        g13 = jnp.repeat(g1, seq, axis=0).reshape(rows // tm2, tm2, D_MODEL)
        g1_spec = pl.BlockSpec((None, tm2, D_MODEL), lambda i: (i, 0, 0))
    x1 = _merge(ya, yb, hg, x2d, g13, g1_spec, weights["wf"], weights["wb"], weights["wo"], tm2)
    qp, xf = _peer_query(x1, weights["norm_ffn"], mod3(sc2), mod3(sh2), mod_spec2, weights["wq"], tm)
    idx, gate = _route(qp, weights["sub_keys"], 256)
    per_tb = seq // tb
    g2_spec = pl.BlockSpec((None, 1, D_MODEL), lambda i: (i // per_tb, 0, 0))
    y = _peer(idx, gate, xf, x1, g2.reshape(batch, 1, D_MODEL), g2_spec, weights["nf"], weights["uv"], tb)
    return y.reshape(batch, seq, D_MODEL), kv, logf_t


def kernel(x_prompt, x_sample, c_prompt, c_sample, cache_fox_k, cache_fox_v, cache_fox_logf, cache_band_k, cache_band_v, w_ada, b_ada, norm_mix, norm_ffn, w_in, b_forget, rel_bias, w_branch_fox, w_branch_band, w_out, w_query, sub_keys, expert_u, expert_v, norm_final):
    bp, sp, _ = x_prompt.shape
    bs, ts, _ = x_sample.shape
    past = cache_fox_k.shape[2]
    win = cache_band_k.shape[2]
    n_exp = expert_u.shape[1]

    w = w_in[0]
    o_ff = 3 * W_ATT
    weights = {
        "w_cat": jnp.concatenate([w[:, :o_ff], w[:, o_ff + N_HEADS:]], axis=1).astype(BF16),
        "wff_t": w[:, o_ff:o_ff + N_HEADS].T,
        "bf_col": b_forget[0].reshape(N_HEADS, 1),
        "norm_mix": norm_mix[0].reshape(1, D_MODEL),
        "norm_ffn": norm_ffn[0].reshape(1, D_MODEL),
        "wf": w_branch_fox[0].astype(BF16),
        "wb": w_branch_band[0].astype(BF16),
        "wo": w_out[0].astype(BF16),
        "wq": w_query[0].astype(BF16),
        "sub_keys": sub_keys[0],
        "nf": norm_final.reshape(1, D_MODEL),
        "uv": jnp.concatenate([expert_u[0].reshape(n_exp, ROW_TILES, LANES),
                               expert_v[0].reshape(n_exp, ROW_TILES, LANES)], axis=1).astype(BF16),
    }
    rb = rel_bias[0]
    bias_own, bias_left = _band_prompt_bias(rb)
    bias_c, bias_n = _band_sample_bias(rb, win, ts)

    ada = _adaln(jnp.concatenate([c_prompt, c_sample], axis=0), w_ada[0], b_ada[0])
    mods_p = [ada[:bp, i * D_MODEL:(i + 1) * D_MODEL] for i in range(6)]
    mods_s = [ada[bp:, i * D_MODEL:(i + 1) * D_MODEL] for i in range(6)]

    def attend_prompt(h, logf_t):
        f = _cumsum_rows(logf_t.reshape(N_HEADS * bp, sp))
        ya = _fox_prompt(h, f, bp, sp)
        yb = _band_prompt(h, bias_own, bias_left, bp, sp)
        return ya, yb

    y_prompt, kv_p, logf_p = _stream(x_prompt, mods_p, weights, attend_prompt, tm=512, tb=64)


    def attend_sample(h, logf_t):
        lf_new = logf_t.reshape(N_HEADS, bs, ts).transpose(1, 0, 2)
        lf_all = jnp.concatenate([cache_fox_logf[0].transpose(0, 2, 1), lf_new], axis=2)
        n_pad = -(past + ts) % CUMSUM_CHUNK
        lf_all = jnp.pad(lf_all, ((0, 0), (0, 0), (0, n_pad)))
        f = _cumsum_rows(lf_all.reshape(bs * N_HEADS, -1)).reshape(bs, N_HEADS, -1)
        ya = _fox_sample(h, cache_fox_k, cache_fox_v, f[:, :, :past], f[:, :, past:past + LANES], bs, ts)
        yb = _band_sample(h, cache_band_k, cache_band_v, bias_c, bias_n, bs, ts)
        return ya, yb

    y_sample, kv_s, logf_s = _stream(x_sample, mods_s, weights, attend_sample, tm=512, tb=ts)

    heads = lambda part, b, t: part.reshape(1, b, t, N_HEADS, HEAD_DIM)
    logf_out = lambda lt, b, t: lt.T.reshape(1, b, t, N_HEADS)
    w_keep = min(BAND_WINDOW, sp)
    return (y_prompt, y_sample,
            heads(kv_p[0], bp, sp), heads(kv_p[1], bp, sp), logf_out(logf_p, bp, sp),
            heads(kv_p[2], bp, sp)[:, :, sp - w_keep:], heads(kv_p[3], bp, sp)[:, :, sp - w_keep:],
            heads(kv_s[0], bs, ts), heads(kv_s[1], bs, ts), logf_out(logf_s, bs, ts),
            heads(kv_s[2], bs, ts), heads(kv_s[3], bs, ts))
```

```python
import functools

import numpy as np
import jax
import jax.numpy as jnp
from jax import lax
from jax.experimental import pallas as pl
from jax.experimental.pallas import tpu as pltpu

D_MODEL = 2048
HEAD_DIM = 128
N_HEADS = 8
W_ATT = N_HEADS * HEAD_DIM
CHUNK = 64
LEFT_CHUNKS = 8
BAND_WINDOW = LEFT_CHUNKS * CHUNK
REL_MAX = 128
N_KEYS = 128
PEER_HEADS = 8
PEER_TOPK = 16
PEER_HALF = 128
PEER_SLOTS = PEER_HEADS * PEER_TOPK
EPS = 1e-6
ATT_SCALE = HEAD_DIM ** -0.5

LANES = 128
SUBLANES = 8
VMEM_LIMIT = 56 * 1024 * 1024
ROW_TILES = D_MODEL // LANES

F32 = jnp.float32
BF16 = jnp.bfloat16
NT_DIMS = (((1,), (1,)), ((), ()))


def _params(*sem):
    return pltpu.CompilerParams(dimension_semantics=sem, vmem_limit_bytes=VMEM_LIMIT)


def _nt(a, b):
    return lax.dot_general(a, b, NT_DIMS, preferred_element_type=F32)


def _norm_mod(x, gain, sc, sh):
    y = x * lax.rsqrt(jnp.mean(x * x, axis=-1, keepdims=True) + EPS)
    return (y * gain) * (1.0 + sc) + sh


def _split_bf16(x):
    hi = x.astype(BF16)
    lo = (x - hi.astype(F32)).astype(BF16)
    return hi, lo


def _adaln_kernel(c_ref, w_ref, b_ref, o_ref):
    c = c_ref[...]
    a = (c * (1.0 / (1.0 + jnp.exp(-c)))).astype(BF16)
    o_ref[...] = jnp.dot(a, w_ref[...].astype(BF16), preferred_element_type=F32) + b_ref[...]


def _adaln(c, w_ada, b_ada):
    rows, tn = c.shape[0], 1024
    n = w_ada.shape[1]
    return pl.pallas_call(
        _adaln_kernel,
        grid=(n // tn,),
        in_specs=[pl.BlockSpec((rows, D_MODEL), lambda j: (0, 0)),
                  pl.BlockSpec((D_MODEL, tn), lambda j: (0, j)),
                  pl.BlockSpec((1, tn), lambda j: (0, j))],
        out_specs=pl.BlockSpec((rows, tn), lambda j: (0, j)),
        out_shape=jax.ShapeDtypeStruct((rows, n), F32),
        compiler_params=_params("arbitrary"),
        name="adaln",
    )(c, w_ada, b_ada.reshape(1, n))


KV_PARTS = (1, 2, 4, 5)
N_ATT_PARTS = 6


INPROJ_NORM_ROWS = 256


def _inproj_kernel(x_ref, gain_ref, sc_ref, sh_ref, w_ref, wff_ref, bf_ref,
                   hb_ref, hg_ref, logf_ref, fk_hbm, fv_hbm, bk_hbm, bv_hbm, xm_scr, stage, sem, *, tm):
    i, j = pl.program_id(0), pl.program_id(1)

    @pl.when(j == 0)
    def _():
        whi, wlo = _split_bf16(wff_ref[...])
        for c in range(tm // INPROJ_NORM_ROWS):
            rs = slice(c * INPROJ_NORM_ROWS, (c + 1) * INPROJ_NORM_ROWS)
            mod = lambda ref: ref[...] if ref.shape[0] == 1 else ref[rs, :]
            xm = _norm_mod(x_ref[rs, :], gain_ref[...], mod(sc_ref), mod(sh_ref))
            hi, lo = _split_bf16(xm)
            xm_scr[rs, :] = hi
            z = _nt(whi, hi) + _nt(whi, lo) + _nt(wlo, hi) + bf_ref[...]
            logf_ref[:, rs] = jnp.minimum(z, 0.0) - jnp.log1p(jnp.exp(-jnp.abs(z)))

    res = jnp.dot(xm_scr[...], w_ref[...], preferred_element_type=F32)

    @pl.when(j < N_ATT_PARTS)
    def _():
        hb_ref[...] = res.astype(BF16)

    @pl.when(j >= N_ATT_PARTS)
    def _():
        hg_ref[...] = res

    outs = (fk_hbm, fv_hbm, bk_hbm, bv_hbm)
    rows = pl.ds(pl.multiple_of(i * tm, tm), tm)

    def head_copies(dst):
        return [pltpu.make_async_copy(stage.at[:, pl.ds(hh * HEAD_DIM, HEAD_DIM)], dst.at[rows, hh, :], sem.at[0])
                for hh in range(N_HEADS)]

    for n, part in enumerate(KV_PARTS):
        @pl.when(j == part)
        def _(n=n):
            if n > 0:
                for cp in head_copies(outs[n - 1]):
                    cp.wait()
            stage[...] = res
            for cp in head_copies(outs[n]):
                cp.start()

    @pl.when(j == pl.num_programs(1) - 1)
    def _():
        for cp in head_copies(outs[-1]):
            cp.wait()


def _inproj(x2d, gain, sc3, sh3, mod_spec, w_cat, wff_t, bf_col, tm):
    rows = x2d.shape[0]
    tn = W_ATT
    nparts = w_cat.shape[1] // tn
    return pl.pallas_call(
        functools.partial(_inproj_kernel, tm=tm),
        grid=(rows // tm, nparts),
        in_specs=[pl.BlockSpec((tm, D_MODEL), lambda i, j: (i, 0), pipeline_mode=pl.Buffered(1)),
                  pl.BlockSpec((1, D_MODEL), lambda i, j: (0, 0)),
                  mod_spec, mod_spec,
                  pl.BlockSpec((D_MODEL, tn), lambda i, j: (0, j)),
                  pl.BlockSpec((N_HEADS, D_MODEL), lambda i, j: (0, 0)),
                  pl.BlockSpec((N_HEADS, 1), lambda i, j: (0, 0))],
        out_specs=[pl.BlockSpec((None, tm, tn), lambda i, j: (jnp.minimum(j, N_ATT_PARTS - 1), i, 0)),
                   pl.BlockSpec((None, tm, tn), lambda i, j: (jnp.maximum(j - N_ATT_PARTS, 0), i, 0)),
                   pl.BlockSpec((N_HEADS, tm), lambda i, j: (0, i)),
                   ] + [pl.BlockSpec(memory_space=pl.ANY)] * len(KV_PARTS),
        out_shape=[jax.ShapeDtypeStruct((N_ATT_PARTS, rows, tn), BF16),
                   jax.ShapeDtypeStruct((nparts - N_ATT_PARTS, rows, tn), F32),
                   jax.ShapeDtypeStruct((N_HEADS, rows), F32),
                   ] + [jax.ShapeDtypeStruct((rows, N_HEADS, HEAD_DIM), F32)] * len(KV_PARTS),
        scratch_shapes=[pltpu.VMEM((tm, D_MODEL), BF16), pltpu.VMEM((tm, W_ATT), F32),
                        pltpu.SemaphoreType.DMA((1,))],
        compiler_params=_params("arbitrary", "arbitrary"),
        name="inproj",
    )(x2d, gain, sc3, sh3, w_cat, wff_t, bf_col)


CUMSUM_CHUNK = 512


def _cumsum_kernel(x_ref, o_ref):
    c = CUMSUM_CHUNK
    r = lax.broadcasted_iota(jnp.int32, (c, c), 0)
    q = lax.broadcasted_iota(jnp.int32, (c, c), 1)
    tri = jnp.where(r <= q, 1.0, 0.0).astype(BF16)
    carry = jnp.zeros((SUBLANES, 1), F32)
    for k in range(x_ref.shape[1] // c):
        x = x_ref[:, k * c:(k + 1) * c]
        hi = x.astype(BF16)
        r1 = x - hi.astype(F32)
        mid = r1.astype(BF16)
        lo = (r1 - mid.astype(F32)).astype(BF16)
        y = (jnp.dot(hi, tri, preferred_element_type=F32) + jnp.dot(mid, tri, preferred_element_type=F32)
             + jnp.dot(lo, tri, preferred_element_type=F32)) + carry
        o_ref[:, k * c:(k + 1) * c] = y
        carry = y[:, c - 1:c]


def _cumsum_rows(x):
    rows, n = x.shape
    return pl.pallas_call(
        _cumsum_kernel,
        grid=(rows // SUBLANES,),
        in_specs=[pl.BlockSpec((SUBLANES, n), lambda i: (i, 0))],
        out_specs=pl.BlockSpec((SUBLANES, n), lambda i: (i, 0)),
        out_shape=jax.ShapeDtypeStruct((rows, n), F32),
        compiler_params=_params("arbitrary"),
        name="cumsum",
    )(x)


def _softmax_update(s, v, m_prev, l_prev, acc_prev):
    m_new = jnp.maximum(m_prev, jnp.max(s, axis=1, keepdims=True))
    alpha = jnp.exp(m_prev - m_new)
    p = jnp.exp(s - m_new)
    l_new = alpha * l_prev + jnp.sum(p, axis=1, keepdims=True)
    acc_new = alpha * acc_prev + jnp.dot(p.astype(BF16), v, preferred_element_type=F32)
    return m_new, l_new, acc_new


FOX_TQ = 512
FOX_RQ = 128


def _fox_prompt_kernel(q_ref, k_ref, v_ref, f_ref, o_ref, m_scr, l_scr, acc_scr):
    tq = FOX_TQ
    qi = pl.program_id(2)
    m_scr[...] = jnp.full(m_scr.shape, -jnp.inf, F32)
    l_scr[...] = jnp.zeros(l_scr.shape, F32)
    acc_scr[...] = jnp.zeros(acc_scr.shape, F32)

    def block(ki, diagonal):
        keys = pl.ds(pl.multiple_of(ki * tq, tq), tq)
        k, v, f = k_ref[keys, :], v_ref[keys, :], f_ref[ki]
        subs = [slice(r * FOX_RQ, (r + 1) * FOX_RQ) for r in range(tq // FOX_RQ)]
        prev = [(m_scr[rs, :], l_scr[rs, :], acc_scr[rs, :]) for rs in subs]
        qk = [_nt(q_ref[rs, :], k) for rs in subs]
        new = []
        for r, rs in enumerate(subs):
            s = qk[r] * ATT_SCALE - f
            if diagonal:
                row = lax.broadcasted_iota(jnp.int32, s.shape, 0) + r * FOX_RQ
                col = lax.broadcasted_iota(jnp.int32, s.shape, 1)
                s = jnp.where(col <= row, s, -jnp.inf)
            new.append(_softmax_update(s, v, *prev[r]))
        for rs, (m, l, acc) in zip(subs, new):
            m_scr[rs, :], l_scr[rs, :], acc_scr[rs, :] = m, l, acc

    def below_diagonal(ki, c):
        block(ki, False)
        return c
    lax.fori_loop(0, qi, below_diagonal, 0)
    block(qi, True)
    o_ref[...] = (acc_scr[...] / l_scr[...]).astype(o_ref.dtype)


def _fox_prompt(h, f, batch, seq):
    tq = FOX_TQ
    nq = seq // tq
    rows = batch * seq
    kv = lambda part: pl.BlockSpec((None, seq, HEAD_DIM), lambda b, hh, qi: (part, b, hh))
    return pl.pallas_call(
        _fox_prompt_kernel,
        grid=(batch, N_HEADS, nq),
        in_specs=[pl.BlockSpec((None, tq, HEAD_DIM), lambda b, hh, qi: (0, b * nq + qi, hh)),
                  kv(1), kv(2),
                  pl.BlockSpec((None, nq, 1, tq), lambda b, hh, qi: (hh * batch + b, 0, 0, 0))],
        out_specs=pl.BlockSpec((tq, HEAD_DIM), lambda b, hh, qi: (b * nq + qi, hh)),
        out_shape=jax.ShapeDtypeStruct((rows, W_ATT), BF16),
        scratch_shapes=[pltpu.VMEM((tq, 1), F32), pltpu.VMEM((tq, 1), F32), pltpu.VMEM((tq, HEAD_DIM), F32)],
        compiler_params=_params("arbitrary", "arbitrary", "arbitrary"),
        name="fox_prompt",
    )(h, h, h, f.reshape(N_HEADS * batch, nq, 1, tq))


BAND_TQ = BAND_WINDOW


def _band_prompt_kernel(q_ref, ko_ref, kl_ref, vo_ref, vl_ref, bo_ref, bl_ref, o_ref):
    ko, kl, vo, vl = ko_ref[...], kl_ref[...], vo_ref[...], vl_ref[...]
    has_left = pl.program_id(2) > 0
    subs = [slice(r * FOX_RQ, (r + 1) * FOX_RQ) for r in range(BAND_TQ // FOX_RQ)]
    qk = [(_nt(q_ref[rs, :], ko), _nt(q_ref[rs, :], kl)) for rs in subs]
    for rs, (qo, ql) in zip(subs, qk):
        so = qo * ATT_SCALE + bo_ref[rs, :]
        sl = jnp.where(has_left, ql * ATT_SCALE + bl_ref[rs, :], -jnp.inf)
        m = jnp.maximum(jnp.max(so, axis=1, keepdims=True), jnp.max(sl, axis=1, keepdims=True))
        po, pp = jnp.exp(so - m), jnp.exp(sl - m)
        l = jnp.sum(po, axis=1, keepdims=True) + jnp.sum(pp, axis=1, keepdims=True)
        acc = (jnp.dot(po.astype(BF16), vo, preferred_element_type=F32)
               + jnp.dot(pp.astype(BF16), vl, preferred_element_type=F32))
        o_ref[rs, :] = (acc / l).astype(o_ref.dtype)


def _band_prompt(h, bias_own, bias_left, batch, seq):
    tq = BAND_TQ
    nq = seq // tq
    own = lambda part: pl.BlockSpec((None, tq, HEAD_DIM), lambda b, hh, qi: (part, b * nq + qi, hh))
    left = lambda part: pl.BlockSpec(
        (None, tq, HEAD_DIM), lambda b, hh, qi: (part, b * nq + jnp.maximum(qi - 1, 0), hh))
    bias = pl.BlockSpec((None, tq, tq), lambda b, hh, qi: (hh, 0, 0))
    return pl.pallas_call(
        _band_prompt_kernel,
        grid=(batch, N_HEADS, nq),
        in_specs=[own(3), own(4), left(4), own(5), left(5), bias, bias],
        out_specs=pl.BlockSpec((tq, HEAD_DIM), lambda b, hh, qi: (b * nq + qi, hh)),
        out_shape=jax.ShapeDtypeStruct((batch * seq, W_ATT), BF16),
        compiler_params=_params("arbitrary", "arbitrary", "arbitrary"),
        name="band_prompt",
    )(h, h, h, h, h, bias_own, bias_left)


def _skew_kernel(f_ref, o_ref):
    fb = jnp.broadcast_to(f_ref[...], o_ref.shape)
    o_ref[...] = pltpu.roll(fb, 0, 1, stride=1, stride_axis=0)


def _rel_bias_toeplitz(rel_bias, n_rows, n_cols, shift):
    hh = rel_bias.shape[0]
    w = -(-(n_rows + n_cols - 1) // LANES) * LANES
    m = np.arange(w)
    m = np.where(m < n_cols, m, m - w)
    f = rel_bias[:, np.clip(shift - m, -REL_MAX, REL_MAX) + REL_MAX].astype(F32)
    table = pl.pallas_call(
        _skew_kernel,
        grid=(hh,),
        in_specs=[pl.BlockSpec((None, 1, w), lambda h: (h, 0, 0))],
        out_specs=pl.BlockSpec((None, n_rows, w), lambda h: (h, 0, 0)),
        out_shape=jax.ShapeDtypeStruct((hh, n_rows, w), F32),
        compiler_params=_params("arbitrary"),
        name="rel_bias_table",
    )(f.reshape(hh, 1, w))
    return table[:, :, :n_cols]


def _band_prompt_bias(rel_bias):
    r = np.arange(BAND_TQ)[:, None]
    c = np.arange(BAND_TQ)[None, :]
    own = jnp.where((c // CHUNK <= r // CHUNK)[None], _rel_bias_toeplitz(rel_bias, BAND_TQ, BAND_TQ, 0), -jnp.inf)
    left = jnp.where((c // CHUNK >= r // CHUNK)[None],
                     _rel_bias_toeplitz(rel_bias, BAND_TQ, BAND_TQ, BAND_TQ), -jnp.inf)
    return own, left


def _fox_sample_kernel(q_ref, kc_hbm, vc_hbm, fc_ref, kn_ref, vn_ref, fn_ref, o_ref,
                       kbuf, vbuf, sem, m_scr, l_scr, acc_scr, *, nk, tk, t_new):
    b, ki = pl.program_id(0), pl.program_id(1)
    n = b * nk + ki
    slot = n % 2

    def cache_copies(bb, kk, sl_):
        start = pl.multiple_of(kk * tk, tk)
        cps = []
        for hh in range(N_HEADS):
            cps.append(pltpu.make_async_copy(kc_hbm.at[0, bb, pl.ds(start, tk), hh, :], kbuf.at[sl_, hh],
                                             sem.at[sl_, 0]))
            cps.append(pltpu.make_async_copy(vc_hbm.at[0, bb, pl.ds(start, tk), hh, :], vbuf.at[sl_, hh],
                                             sem.at[sl_, 1]))
        return cps

    @pl.when(n == 0)
    def _():
        for cp in cache_copies(b, ki, slot):
            cp.start()

    @pl.when(n + 1 < pl.num_programs(0) * nk)
    def _():
        last = ki == nk - 1
        for cp in cache_copies(jnp.where(last, b + 1, b), jnp.where(last, 0, ki + 1), 1 - slot):
            cp.start()

    @pl.when(ki == 0)
    def _():
        m_scr[...] = jnp.full(m_scr.shape, -jnp.inf, F32)
        l_scr[...] = jnp.zeros(l_scr.shape, F32)
        acc_scr[...] = jnp.zeros(acc_scr.shape, F32)

    for cp in cache_copies(b, ki, slot):
        cp.wait()

    for hh in range(N_HEADS):
        sl = slice(hh * HEAD_DIM, (hh + 1) * HEAD_DIM)
        s = _nt(q_ref[:, sl], kbuf[slot, hh].astype(BF16)) * ATT_SCALE - fc_ref[hh:hh + 1, :]
        m, l, acc = _softmax_update(s, vbuf[slot, hh].astype(BF16), m_scr[hh], l_scr[hh], acc_scr[hh])
        m_scr[hh], l_scr[hh], acc_scr[hh] = m, l, acc

    @pl.when(ki == nk - 1)
    def _():
        for hh in range(N_HEADS):
            sl = slice(hh * HEAD_DIM, (hh + 1) * HEAD_DIM)
            s = _nt(q_ref[:, sl], kn_ref[:, sl]) * ATT_SCALE - fn_ref[hh:hh + 1, 0:t_new]
            row = lax.broadcasted_iota(jnp.int32, s.shape, 0)
            col = lax.broadcasted_iota(jnp.int32, s.shape, 1)
            s = jnp.where(col <= row, s, -jnp.inf)
            m, l, acc = _softmax_update(s, vn_ref[:, sl], m_scr[hh], l_scr[hh], acc_scr[hh])
            o_ref[:, sl] = (acc / l).astype(o_ref.dtype)


def _fox_sample(h, cache_k, cache_v, f_cache, f_new, batch, t_new, tk=1024):
    past = cache_k.shape[2]
    nk = past // tk
    new = lambda part: pl.BlockSpec((None, t_new, W_ATT), lambda b, ki: (part, b, 0))
    cache = pl.BlockSpec(memory_space=pl.ANY)
    head_major = pltpu.VMEM((2, N_HEADS, tk, HEAD_DIM), cache_k.dtype)
    return pl.pallas_call(
        functools.partial(_fox_sample_kernel, nk=nk, tk=tk, t_new=t_new),
        grid=(batch, nk),
        in_specs=[new(0), cache, cache,
                  pl.BlockSpec((None, N_HEADS, tk), lambda b, ki: (b, 0, ki)),
                  new(1), new(2),
                  pl.BlockSpec((None, N_HEADS, LANES), lambda b, ki: (b, 0, 0))],
        out_specs=pl.BlockSpec((t_new, W_ATT), lambda b, ki: (b, 0)),
        out_shape=jax.ShapeDtypeStruct((batch * t_new, W_ATT), BF16),
        scratch_shapes=[head_major, head_major, pltpu.SemaphoreType.DMA((2, 2)),
                        pltpu.VMEM((N_HEADS, t_new, 1), F32), pltpu.VMEM((N_HEADS, t_new, 1), F32),
                        pltpu.VMEM((N_HEADS, t_new, HEAD_DIM), F32)],
        compiler_params=_params("arbitrary", "arbitrary"),
        name="fox_sample",
    )(h, cache_k, cache_v, f_cache, h, h, f_new)


def _band_sample_kernel(q_ref, kc_ref, vc_ref, kn_ref, vn_ref, bc_ref, bn_ref, o_ref):
    for hh in range(N_HEADS):
        sl = slice(hh * HEAD_DIM, (hh + 1) * HEAD_DIM)
        q = q_ref[:, sl]
        sc = _nt(q, kc_ref[:, hh, :].astype(BF16)) * ATT_SCALE + bc_ref[hh]
        sn = _nt(q, kn_ref[:, sl]) * ATT_SCALE + bn_ref[hh]
        m = jnp.maximum(jnp.max(sc, axis=1, keepdims=True), jnp.max(sn, axis=1, keepdims=True))
        pc, pn = jnp.exp(sc - m), jnp.exp(sn - m)
        l = jnp.sum(pc, axis=1, keepdims=True) + jnp.sum(pn, axis=1, keepdims=True)
        acc = (jnp.dot(pc.astype(BF16), vc_ref[:, hh, :].astype(BF16), preferred_element_type=F32)
               + jnp.dot(pn.astype(BF16), vn_ref[:, sl], preferred_element_type=F32))
        o_ref[:, sl] = (acc / l).astype(o_ref.dtype)


def _band_sample(h, cache_k, cache_v, bias_c, bias_n, batch, t_new):
    win = cache_k.shape[2]
    new = lambda part: pl.BlockSpec((None, t_new, W_ATT), lambda b: (part, b, 0))
    cache = pl.BlockSpec((None, None, win, N_HEADS, HEAD_DIM), lambda b: (0, b, 0, 0, 0))
    return pl.pallas_call(
        _band_sample_kernel,
        grid=(batch,),
        in_specs=[new(3), cache, cache, new(4), new(5),
                  pl.BlockSpec((N_HEADS, t_new, win), lambda b: (0, 0, 0)),
                  pl.BlockSpec((N_HEADS, t_new, t_new), lambda b: (0, 0, 0))],
        out_specs=pl.BlockSpec((t_new, W_ATT), lambda b: (b, 0)),
        out_shape=jax.ShapeDtypeStruct((batch * t_new, W_ATT), BF16),
        compiler_params=_params("arbitrary"),
        name="band_sample",
    )(h, cache_k, cache_v, h, h, bias_c, bias_n)


def _band_sample_bias(rel_bias, win, t_new):
    b = _rel_bias_toeplitz(rel_bias, t_new, win + t_new, win)
    return b[:, :, :win], b[:, :, win:]


def _merge_kernel(ya_ref, yb_ref, ga0_ref, ga1_ref, gb0_ref, gb1_ref, x_ref, g1_ref,
                  wf_ref, wb_ref, wo_ref, o_ref):
    a = jnp.dot(ya_ref[...], wf_ref[...], preferred_element_type=F32)
    b = jnp.dot(yb_ref[...], wb_ref[...], preferred_element_type=F32)
    sig = lambda z: 1.0 / (1.0 + jnp.exp(-z))
    half = D_MODEL // 2
    m0 = (sig(ga0_ref[...]) * a[:, :half] + sig(gb0_ref[...]) * b[:, :half]).astype(BF16)
    m1 = (sig(ga1_ref[...]) * a[:, half:] + sig(gb1_ref[...]) * b[:, half:]).astype(BF16)
    y = (jnp.dot(m0, wo_ref[:half, :], preferred_element_type=F32)
         + jnp.dot(m1, wo_ref[half:, :], preferred_element_type=F32))
    o_ref[...] = x_ref[...] + g1_ref[...] * y


def _merge(ya, yb, h, x2d, g13, mod_spec, wf, wb, wo, tm):
    rows = x2d.shape[0]
    part = lambda p: pl.BlockSpec((None, tm, W_ATT), lambda i: (p, i, 0))
    const = lambda shape: pl.BlockSpec(shape, lambda i: (0, 0), pipeline_mode=pl.Buffered(1))
    return pl.pallas_call(
        _merge_kernel,
        grid=(rows // tm,),
        in_specs=[pl.BlockSpec((tm, W_ATT), lambda i: (i, 0)), pl.BlockSpec((tm, W_ATT), lambda i: (i, 0)),
                  part(0), part(1), part(2), part(3),
                  pl.BlockSpec((tm, D_MODEL), lambda i: (i, 0)),
                  mod_spec,
                  const((W_ATT, D_MODEL)), const((W_ATT, D_MODEL)), const((D_MODEL, D_MODEL))],
        out_specs=pl.BlockSpec((tm, D_MODEL), lambda i: (i, 0)),
        out_shape=jax.ShapeDtypeStruct((rows, D_MODEL), F32),
        compiler_params=_params("arbitrary"),
        name="merge",
    )(ya, yb, h, h, h, h, x2d, g13, wf, wb, wo)


def _peer_query_kernel(x_ref, gain_ref, sc_ref, sh_ref, w_ref, q_ref, xf_ref, xm_scr):
    @pl.when(pl.program_id(1) == 0)
    def _():
        xf = _norm_mod(x_ref[...], gain_ref[...], sc_ref[...], sh_ref[...])
        xf_ref[...] = xf
        xm_scr[...] = xf.astype(BF16)

    q_ref[...] = jnp.dot(xm_scr[...], w_ref[...], preferred_element_type=F32)


def _peer_query(x2d, gain, sc3, sh3, mod_spec, wq, tm):
    rows = x2d.shape[0]
    tn = wq.shape[1]
    return pl.pallas_call(
        _peer_query_kernel,
        grid=(rows // tm, wq.shape[1] // tn),
        in_specs=[pl.BlockSpec((tm, D_MODEL), lambda i, j: (i, 0)),
                  pl.BlockSpec((1, D_MODEL), lambda i, j: (0, 0)),
                  mod_spec, mod_spec,
                  pl.BlockSpec((D_MODEL, tn), lambda i, j: (0, j), pipeline_mode=pl.Buffered(1))],
        out_specs=[pl.BlockSpec((tm, tn), lambda i, j: (i, j)),
                   pl.BlockSpec((tm, D_MODEL), lambda i, j: (i, 0))],
        out_shape=[jax.ShapeDtypeStruct((rows, wq.shape[1]), F32),
                   jax.ShapeDtypeStruct((rows, D_MODEL), F32)],
        scratch_shapes=[pltpu.VMEM((tm, D_MODEL), BF16)],
        compiler_params=_params("arbitrary", "arbitrary"),
        name="peer_query",
    )(x2d, gain, sc3, sh3, wq)


def _top_k_rows(s, k, ids=None):
    iota = lax.broadcasted_iota(jnp.int32, s.shape, 0) if ids is None else ids
    n = jnp.iinfo(jnp.int32).max
    vals, idxs = [], []
    for _ in range(k):
        m = jnp.max(s, axis=0, keepdims=True)
        i = jnp.min(jnp.where(s == m, iota, n), axis=0, keepdims=True)
        vals.append(m)
        idxs.append(i)
        s = jnp.where(iota == i, -jnp.inf, s)
    return jnp.concatenate(vals, axis=0), jnp.concatenate(idxs, axis=0)


def _pair_candidates(s1, s2, k):
    assert k // 2 == SUBLANES
    row = lax.broadcasted_iota(jnp.int32, (SUBLANES, s1.shape[1]), 0)
    cands, ids = [], []
    for a in range(k // 2):
        nb = k // (a + 1)
        for b0 in range(0, nb, SUBLANES):
            piece = s1[a:a + 1, :] + s2[b0:b0 + SUBLANES, :]
            if nb - b0 < SUBLANES:
                piece = jnp.where(row < nb - b0, piece, -jnp.inf)
            cands.append(piece)
            ids.append(a * k + b0 + row)
    cands.append(s1[k // 2:k, :] + s2[0:1, :])
    ids.append((k // 2 + row) * k)
    return jnp.concatenate(cands, axis=0), jnp.concatenate(ids, axis=0)


def _select_rows(table, sel):
    out = jnp.zeros(sel.shape, table.dtype)
    for r in range(table.shape[0]):
        out = jnp.where(sel == r, table[r:r + 1, :], out)
    return out


def _route_kernel(q_ref, sk_ref, e_ref, g_ref):
    k = PEER_TOPK
    e_all, g_all = [], []
    for hh in range(PEER_HEADS):
        halves = []
        for c in range(2):
            col = (2 * hh + c) * PEER_HALF
            qh = q_ref[:, col:col + PEER_HALF].astype(BF16)
            halves.append(_top_k_rows(_nt(sk_ref[hh, c].astype(BF16), qh), k))
        (s1, i1), (s2, i2) = halves
        cand, cand_ids = _pair_candidates(s1, s2, k)
        top_s, top_c = _top_k_rows(cand, k, cand_ids)
        e_all.append(_select_rows(i1, top_c >> (k.bit_length() - 1)) * N_KEYS + _select_rows(i2, top_c & (k - 1)))
        p = jnp.exp(top_s - top_s[0:1, :])
        g_all.append(p / jnp.sum(p, axis=0, keepdims=True))
    e_ref[...] = jnp.concatenate(e_all, axis=0).T
    g_ref[...] = jnp.concatenate(g_all, axis=0).T


def _route(qp, sub_keys, tt):
    rows = qp.shape[0]
    return pl.pallas_call(
        _route_kernel,
        grid=(rows // tt,),
        in_specs=[pl.BlockSpec((tt, qp.shape[1]), lambda i: (i, 0)),
                  pl.BlockSpec(sub_keys.shape, lambda i: (0, 0, 0, 0))],
        out_specs=[pl.BlockSpec((tt, PEER_SLOTS), lambda i: (i, 0)),
                   pl.BlockSpec((tt, PEER_SLOTS), lambda i: (i, 0))],
        out_shape=[jax.ShapeDtypeStruct((rows, PEER_SLOTS), jnp.int32),
                   jax.ShapeDtypeStruct((rows, PEER_SLOTS), F32)],
        compiler_params=_params("arbitrary"),
        name="peer_route",
    )(qp, sub_keys)


PEER_TOK = 4
PEER_NBUF = 4
PEER_AHEAD = 2
UV_ROWS = 2 * ROW_TILES
PEER_GROUP = SUBLANES
PEER_NGROUP = PEER_SLOTS // PEER_GROUP


def _peer_kernel(idx_ref, g_ref, xf_ref, x1_ref, g2_ref, nf_ref, uv_ref, y_ref,
                 buf0, buf1, buf2, buf3, p_scr, wb_scr, xf3_scr, o3_scr, sem, *, tb):
    bufs = (buf0, buf1, buf2, buf3)
    n_stage = tb // PEER_TOK
    step = pl.program_id(0)

    def issue(tok, b, k, j):
        e = idx_ref[0, tok * PEER_SLOTS + j]
        pltpu.make_async_copy(uv_ref.at[e], bufs[b].at[k, j], sem.at[b, k]).start()

    def wait(b, k):
        pltpu.make_async_copy(uv_ref.at[pl.ds(0, PEER_SLOTS)], bufs[b].at[k], sem.at[b, k]).wait()

    @pl.when(step == 0)
    def _():
        def prologue(jo, c):
            for ji in range(PEER_GROUP):
                for b in range(PEER_AHEAD):
                    for k in range(PEER_TOK):
                        issue(b * PEER_TOK + k, b, k, jo * PEER_GROUP + ji)
            return c
        lax.fori_loop(0, PEER_NGROUP, prologue, 0)

    for c in range(ROW_TILES):
        xf3_scr[:, c, :] = xf_ref[:, c * LANES:(c + 1) * LANES]

    def stage(s, b):
        buf, nb = bufs[b], (b + PEER_AHEAD) % PEER_NBUF
        t0 = s * PEER_TOK
        tn = t0 + PEER_AHEAD * PEER_TOK
        for k in range(PEER_TOK):
            wait(b, k)
        xs = [(xf3_scr[t0 + k, 0:SUBLANES, :], xf3_scr[t0 + k, SUBLANES:ROW_TILES, :])
              for k in range(PEER_TOK)]

        def dot_body(jo, c):
            for ji in range(PEER_GROUP):
                j = jo * PEER_GROUP + ji
                issue(tn, nb, 0, j)
                issue(tn + 1, nb, 1, j)
                row = pl.multiple_of(j * SUBLANES, SUBLANES)
                for k in range(PEER_TOK):
                    u = buf[k, j, 0:ROW_TILES, :].astype(F32)
                    p = u[0:SUBLANES] * xs[k][0] + u[SUBLANES:ROW_TILES] * xs[k][1]
                    p_scr[k, pl.ds(row, SUBLANES), :] = p
            return c
        lax.fori_loop(0, PEER_NGROUP, dot_body, 0)

        for k in range(PEER_TOK):
            ps = p_scr[k, pl.ds(0, PEER_SLOTS, stride=SUBLANES), :]
            for r in range(1, SUBLANES):
                ps = ps + p_scr[k, pl.ds(r, PEER_SLOTS, stride=SUBLANES), :]
            h = jnp.sum(ps.T, axis=0, keepdims=True)
            w = g_ref[pl.ds(t0 + k, 1), :] * (0.5 * h * (1.0 + lax.erf(h * (2.0 ** -0.5))))
            wb_scr[k] = jnp.broadcast_to(w, (PEER_SLOTS, PEER_SLOTS)).T

        def acc_body(jo, accs):
            accs = list(accs)
            for ji in range(PEER_GROUP):
                j = jo * PEER_GROUP + ji
                issue(tn + 2, nb, 2, j)
                issue(tn + 3, nb, 3, j)
                for k in range(PEER_TOK):
                    wv = wb_scr[k, pl.ds(j, 1), :]
                    v = buf[k, j, ROW_TILES:UV_ROWS, :].astype(F32)
                    accs[2 * k] = accs[2 * k] + v[0:SUBLANES] * wv
                    accs[2 * k + 1] = accs[2 * k + 1] + v[SUBLANES:ROW_TILES] * wv
            return tuple(accs)
        zero = jnp.zeros((SUBLANES, LANES), F32)
        accs = lax.fori_loop(0, PEER_NGROUP, acc_body, (zero,) * (2 * PEER_TOK))
        for k in range(PEER_TOK):
            o3_scr[t0 + k, 0:SUBLANES, :] = accs[2 * k]
            o3_scr[t0 + k, SUBLANES:ROW_TILES, :] = accs[2 * k + 1]

    def rotation(q, c):
        for b in range(PEER_NBUF):
            stage(q * PEER_NBUF + b, b)
        return c
    lax.fori_loop(0, n_stage // PEER_NBUF, rotation, 0)

    @pl.when(step == pl.num_programs(0) - 1)
    def _():
        for b in range(PEER_AHEAD):
            for k in range(PEER_TOK):
                wait(b, k)

    ss = jnp.zeros((tb, 1), F32)
    for c in range(ROW_TILES):
        sl = slice(c * LANES, (c + 1) * LANES)
        z = x1_ref[:, sl] + g2_ref[:, sl] * o3_scr[:, c, :]
        y_ref[:, sl] = z
        ss = ss + jnp.sum(z * z, axis=1, keepdims=True)
    inv = lax.rsqrt(ss * (1.0 / D_MODEL) + EPS)
    for c in range(ROW_TILES):
        sl = slice(c * LANES, (c + 1) * LANES)
        y_ref[:, sl] = y_ref[:, sl] * inv * nf_ref[:, sl]


def _peer(idx, gate, xf, x1, g23, g2_spec, nf, uv, tb):
    rows = xf.shape[0]
    nblk = rows // tb
    tok = lambda: pl.BlockSpec((tb, D_MODEL), lambda i: (i, 0))
    n_ahead = PEER_AHEAD * PEER_TOK * PEER_SLOTS
    idx2 = idx.reshape(nblk, tb * PEER_SLOTS)
    nxt = jnp.concatenate([idx2[1:, :n_ahead], jnp.zeros((1, n_ahead), idx.dtype)], axis=0)
    idx_ext = jnp.concatenate([idx2, nxt], axis=1).reshape(nblk, 1, tb * PEER_SLOTS + n_ahead)
    gather_buf = pltpu.VMEM((PEER_TOK, PEER_SLOTS, UV_ROWS, LANES), uv.dtype)
    return pl.pallas_call(
        functools.partial(_peer_kernel, tb=tb),
        grid=(nblk,),
        in_specs=[pl.BlockSpec((None, 1, tb * PEER_SLOTS + n_ahead), lambda i: (i, 0, 0),
                               memory_space=pltpu.SMEM),
                  pl.BlockSpec((tb, PEER_SLOTS), lambda i: (i, 0)),
                  tok(), tok(), g2_spec,
                  pl.BlockSpec((1, D_MODEL), lambda i: (0, 0)),
                  pl.BlockSpec(memory_space=pl.ANY)],
        out_specs=tok(),
        out_shape=jax.ShapeDtypeStruct((rows, D_MODEL), F32),
        scratch_shapes=[gather_buf] * PEER_NBUF + [
                        pltpu.VMEM((PEER_TOK, PEER_SLOTS * SUBLANES, LANES), F32),
                        pltpu.VMEM((PEER_TOK, PEER_SLOTS, LANES), F32),
                        pltpu.VMEM((tb, ROW_TILES, LANES), F32),
                        pltpu.VMEM((tb, ROW_TILES, LANES), F32),
                        pltpu.SemaphoreType.DMA((PEER_NBUF, PEER_TOK))],
        compiler_params=_params("arbitrary"),
        name="peer_experts",
    )(idx_ext, gate, xf, x1, g23, nf, uv)


def _stream(x, mods, weights, attend, tm_in, tm, tb):
    batch, seq, _ = x.shape
    rows = batch * seq
    tm, tm_in = min(tm, rows), min(tm_in, rows)
    sh1, sc1, g1, sh2, sc2, g2 = mods
    x2d = x.reshape(rows, D_MODEL)

    def modulation(tm_):
        if seq >= tm_:
            per = seq // tm_
            return (lambda m: m.reshape(batch, 1, D_MODEL),
                    pl.BlockSpec((None, 1, D_MODEL), lambda i, j: (i // per, 0, 0)))
        return (lambda m: jnp.repeat(m, seq, axis=0).reshape(rows // tm_, tm_, D_MODEL),
                pl.BlockSpec((None, tm_, D_MODEL), lambda i, j: (i, 0, 0)))

    mod3_in, mod_spec_in = modulation(tm_in)
    mod3, mod_spec2 = modulation(tm)
    hb, hg, logf_t, *kv = _inproj(x2d, weights["norm_mix"], mod3_in(sc1), mod3_in(sh1), mod_spec_in,
                                  weights["w_cat"], weights["wff_t"], weights["bf_col"], tm_in)
    ya, yb = attend(hb, logf_t)
    tm2 = min(tm, 256)
    if seq >= tm2:
        per2 = seq // tm2
        g13 = g1.reshape(batch, 1, D_MODEL)
        g1_spec = pl.BlockSpec((None, 1, D_MODEL), lambda i: (i // per2, 0, 0))
    else:
        g13 = jnp.repeat(g1, seq, axis=0).reshape(rows // tm2, tm2, D_MODEL)
        g1_spec = pl.BlockSpec((None, tm2, D_MODEL), lambda i: (i, 0, 0))
    x1 = _merge(ya, yb, hg, x2d, g13, g1_spec, weights["wf"], weights["wb"], weights["wo"], tm2)
    qp, xf = _peer_query(x1, weights["norm_ffn"], mod3(sc2), mod3(sh2), mod_spec2, weights["wq"], tm)
    idx, gate = _route(qp, weights["sub_keys"], 256)
    per_tb = seq // tb
    g2_spec = pl.BlockSpec((None, 1, D_MODEL), lambda i: (i // per_tb, 0, 0))
    y = _peer(idx, gate, xf, x1, g2.reshape(batch, 1, D_MODEL), g2_spec, weights["nf"], weights["uv"], tb)
    return y.reshape(batch, seq, D_MODEL), kv, logf_t


def kernel(x_prompt, x_sample, c_prompt, c_sample, cache_fox_k, cache_fox_v, cache_fox_logf, cache_band_k, cache_band_v, w_ada, b_ada, norm_mix, norm_ffn, w_in, b_forget, rel_bias, w_branch_fox, w_branch_band, w_out, w_query, sub_keys, expert_u, expert_v, norm_final):
    bp, sp, _ = x_prompt.shape
    bs, ts, _ = x_sample.shape
    past = cache_fox_k.shape[2]
    win = cache_band_k.shape[2]
    n_exp = expert_u.shape[1]

    w = w_in[0]
    o_ff = 3 * W_ATT
    weights = {
        "w_cat": jnp.concatenate([w[:, :o_ff], w[:, o_ff + N_HEADS:]], axis=1).astype(BF16),
        "wff_t": w[:, o_ff:o_ff + N_HEADS].T,
        "bf_col": b_forget[0].reshape(N_HEADS, 1),
        "norm_mix": norm_mix[0].reshape(1, D_MODEL),
        "norm_ffn": norm_ffn[0].reshape(1, D_MODEL),
        "wf": w_branch_fox[0].astype(BF16),
        "wb": w_branch_band[0].astype(BF16),
        "wo": w_out[0].astype(BF16),
        "wq": w_query[0].astype(BF16),
        "sub_keys": sub_keys[0],
        "nf": norm_final.reshape(1, D_MODEL),
        "uv": jnp.concatenate([expert_u[0].reshape(n_exp, ROW_TILES, LANES),
                               expert_v[0].reshape(n_exp, ROW_TILES, LANES)], axis=1).astype(BF16),
    }
    rb = rel_bias[0]
    bias_own, bias_left = _band_prompt_bias(rb)
    bias_c, bias_n = _band_sample_bias(rb, win, ts)

    ada = _adaln(jnp.concatenate([c_prompt, c_sample], axis=0), w_ada[0], b_ada[0])
    mods_p = [ada[:bp, i * D_MODEL:(i + 1) * D_MODEL] for i in range(6)]
    mods_s = [ada[bp:, i * D_MODEL:(i + 1) * D_MODEL] for i in range(6)]

    def attend_prompt(h, logf_t):
        f = _cumsum_rows(logf_t.reshape(N_HEADS * bp, sp))
        ya = _fox_prompt(h, f, bp, sp)
        yb = _band_prompt(h, bias_own, bias_left, bp, sp)
        return ya, yb

    y_prompt, kv_p, logf_p = _stream(x_prompt, mods_p, weights, attend_prompt, tm_in=1024, tm=512, tb=64)


    def attend_sample(h, logf_t):
        lf_new = logf_t.reshape(N_HEADS, bs, ts).transpose(1, 0, 2)
        lf_all = jnp.concatenate([cache_fox_logf[0].transpose(0, 2, 1), lf_new], axis=2)
        n_pad = -(past + ts) % CUMSUM_CHUNK
        lf_all = jnp.pad(lf_all, ((0, 0), (0, 0), (0, n_pad)))
        f = _cumsum_rows(lf_all.reshape(bs * N_HEADS, -1)).reshape(bs, N_HEADS, -1)
        ya = _fox_sample(h, cache_fox_k, cache_fox_v, f[:, :, :past], f[:, :, past:past + LANES], bs, ts)
        yb = _band_sample(h, cache_band_k, cache_band_v, bias_c, bias_n, bs, ts)
        return ya, yb

    y_sample, kv_s, logf_s = _stream(x_sample, mods_s, weights, attend_sample, tm_in=512, tm=512, tb=ts)

    heads = lambda part, b, t: part.reshape(1, b, t, N_HEADS, HEAD_DIM)
    logf_out = lambda lt, b, t: lt.T.reshape(1, b, t, N_HEADS)
    w_keep = min(BAND_WINDOW, sp)
    return (y_prompt, y_sample,
            heads(kv_p[0], bp, sp), heads(kv_p[1], bp, sp), logf_out(logf_p, bp, sp),
            heads(kv_p[2], bp, sp)[:, :, sp - w_keep:], heads(kv_p[3], bp, sp)[:, :, sp - w_keep:],
            heads(kv_s[0], bs, ts), heads(kv_s[1], bs, ts), logf_out(logf_s, bs, ts),
            heads(kv_s[2], bs, ts), heads(kv_s[3], bs, ts))
```

```python
import functools

import numpy as np
import jax
import jax.numpy as jnp
from jax import lax
from jax.experimental import pallas as pl
from jax.experimental.pallas import tpu as pltpu

D_MODEL = 2048
HEAD_DIM = 128
N_HEADS = 8
W_ATT = N_HEADS * HEAD_DIM
CHUNK = 64
LEFT_CHUNKS = 8
BAND_WINDOW = LEFT_CHUNKS * CHUNK
REL_MAX = 128
N_KEYS = 128
PEER_HEADS = 8
PEER_TOPK = 16
PEER_HALF = 128
PEER_SLOTS = PEER_HEADS * PEER_TOPK
EPS = 1e-6
ATT_SCALE = HEAD_DIM ** -0.5

LANES = 128
SUBLANES = 8
VMEM_LIMIT = 56 * 1024 * 1024
ROW_TILES = D_MODEL // LANES

F32 = jnp.float32
BF16 = jnp.bfloat16
NT_DIMS = (((1,), (1,)), ((), ()))


def _params(*sem):
    return pltpu.CompilerParams(dimension_semantics=sem, vmem_limit_bytes=VMEM_LIMIT)


def _nt(a, b):
    return lax.dot_general(a, b, NT_DIMS, preferred_element_type=F32)


def _norm_mod(x, gain, sc, sh):
    y = x * lax.rsqrt(jnp.mean(x * x, axis=-1, keepdims=True) + EPS)
    return (y * gain) * (1.0 + sc) + sh


def _split_bf16(x):
    hi = x.astype(BF16)
    lo = (x - hi.astype(F32)).astype(BF16)
    return hi, lo


def _adaln_kernel(c_ref, w_ref, b_ref, o_ref):
    c = c_ref[...]
    a = (c * (1.0 / (1.0 + jnp.exp(-c)))).astype(BF16)
    o_ref[...] = jnp.dot(a, w_ref[...].astype(BF16), preferred_element_type=F32) + b_ref[...]


def _adaln(c, w_ada, b_ada):
    rows, tn = c.shape[0], 1024
    n = w_ada.shape[1]
    return pl.pallas_call(
        _adaln_kernel,
        grid=(n // tn,),
        in_specs=[pl.BlockSpec((rows, D_MODEL), lambda j: (0, 0)),
                  pl.BlockSpec((D_MODEL, tn), lambda j: (0, j)),
                  pl.BlockSpec((1, tn), lambda j: (0, j))],
        out_specs=pl.BlockSpec((rows, tn), lambda j: (0, j)),
        out_shape=jax.ShapeDtypeStruct((rows, n), F32),
        compiler_params=_params("arbitrary"),
        name="adaln",
    )(c, w_ada, b_ada.reshape(1, n))


KV_PARTS = (1, 2, 4, 5)
N_ATT_PARTS = 6


INPROJ_NORM_ROWS = 256


def _inproj_kernel(x_ref, gain_ref, sc_ref, sh_ref, w_ref, wff_ref, bf_ref,
                   hb_ref, hg_ref, logf_ref, fk_hbm, fv_hbm, bk_hbm, bv_hbm, xm_scr, stage, sem, *, tm):
    i, j = pl.program_id(0), pl.program_id(1)

    @pl.when(j == 0)
    def _():
        whi, wlo = _split_bf16(wff_ref[...])
        for c in range(tm // INPROJ_NORM_ROWS):
            rs = slice(c * INPROJ_NORM_ROWS, (c + 1) * INPROJ_NORM_ROWS)
            mod = lambda ref: ref[...] if ref.shape[0] == 1 else ref[rs, :]
            xm = _norm_mod(x_ref[rs, :], gain_ref[...], mod(sc_ref), mod(sh_ref))
            hi, lo = _split_bf16(xm)
            xm_scr[rs, :] = hi
            z = _nt(whi, hi) + _nt(whi, lo) + _nt(wlo, hi) + bf_ref[...]
            logf_ref[:, rs] = jnp.minimum(z, 0.0) - jnp.log1p(jnp.exp(-jnp.abs(z)))

    res = jnp.dot(xm_scr[...], w_ref[...], preferred_element_type=F32)

    @pl.when(j < N_ATT_PARTS)
    def _():
        hb_ref[...] = res.astype(BF16)

    @pl.when(j >= N_ATT_PARTS)
    def _():
        hg_ref[...] = res

    outs = (fk_hbm, fv_hbm, bk_hbm, bv_hbm)
    rows = pl.ds(pl.multiple_of(i * tm, tm), tm)

    def head_copies(dst):
        return [pltpu.make_async_copy(stage.at[:, pl.ds(hh * HEAD_DIM, HEAD_DIM)], dst.at[rows, hh, :], sem.at[0])
                for hh in range(N_HEADS)]

    for n, part in enumerate(KV_PARTS):
        @pl.when(j == part)
        def _(n=n):
            if n > 0:
                for cp in head_copies(outs[n - 1]):
                    cp.wait()
            stage[...] = res
            for cp in head_copies(outs[n]):
                cp.start()

    @pl.when(j == pl.num_programs(1) - 1)
    def _():
        for cp in head_copies(outs[-1]):
            cp.wait()


def _inproj(x2d, gain, sc3, sh3, mod_spec, w_cat, wff_t, bf_col, tm):
    rows = x2d.shape[0]
    tn = W_ATT
    nparts = w_cat.shape[1] // tn
    return pl.pallas_call(
        functools.partial(_inproj_kernel, tm=tm),
        grid=(rows // tm, nparts),
        in_specs=[pl.BlockSpec((tm, D_MODEL), lambda i, j: (i, 0), pipeline_mode=pl.Buffered(1)),
                  pl.BlockSpec((1, D_MODEL), lambda i, j: (0, 0)),
                  mod_spec, mod_spec,
                  pl.BlockSpec((D_MODEL, tn), lambda i, j: (0, j)),
                  pl.BlockSpec((N_HEADS, D_MODEL), lambda i, j: (0, 0)),
                  pl.BlockSpec((N_HEADS, 1), lambda i, j: (0, 0))],
        out_specs=[pl.BlockSpec((None, tm, tn), lambda i, j: (jnp.minimum(j, N_ATT_PARTS - 1), i, 0)),
                   pl.BlockSpec((None, tm, tn), lambda i, j: (jnp.maximum(j - N_ATT_PARTS, 0), i, 0)),
                   pl.BlockSpec((N_HEADS, tm), lambda i, j: (0, i)),
                   ] + [pl.BlockSpec(memory_space=pl.ANY)] * len(KV_PARTS),
        out_shape=[jax.ShapeDtypeStruct((N_ATT_PARTS, rows, tn), BF16),
                   jax.ShapeDtypeStruct((nparts - N_ATT_PARTS, rows, tn), F32),
                   jax.ShapeDtypeStruct((N_HEADS, rows), F32),
                   ] + [jax.ShapeDtypeStruct((rows, N_HEADS, HEAD_DIM), F32)] * len(KV_PARTS),
        scratch_shapes=[pltpu.VMEM((tm, D_MODEL), BF16), pltpu.VMEM((tm, W_ATT), F32),
                        pltpu.SemaphoreType.DMA((1,))],
        compiler_params=_params("arbitrary", "arbitrary"),
        name="inproj",
    )(x2d, gain, sc3, sh3, w_cat, wff_t, bf_col)


CUMSUM_CHUNK = 512


def _cumsum_kernel(x_ref, o_ref):
    c = CUMSUM_CHUNK
    r = lax.broadcasted_iota(jnp.int32, (c, c), 0)
    q = lax.broadcasted_iota(jnp.int32, (c, c), 1)
    tri = jnp.where(r <= q, 1.0, 0.0).astype(BF16)
    carry = jnp.zeros((SUBLANES, 1), F32)
    for k in range(x_ref.shape[1] // c):
        x = x_ref[:, k * c:(k + 1) * c]
        hi = x.astype(BF16)
        r1 = x - hi.astype(F32)
        mid = r1.astype(BF16)
        lo = (r1 - mid.astype(F32)).astype(BF16)
        y = (jnp.dot(hi, tri, preferred_element_type=F32) + jnp.dot(mid, tri, preferred_element_type=F32)
             + jnp.dot(lo, tri, preferred_element_type=F32)) + carry
        o_ref[:, k * c:(k + 1) * c] = y
        carry = y[:, c - 1:c]


def _cumsum_rows(x):
    rows, n = x.shape
    return pl.pallas_call(
        _cumsum_kernel,
        grid=(rows // SUBLANES,),
        in_specs=[pl.BlockSpec((SUBLANES, n), lambda i: (i, 0))],
        out_specs=pl.BlockSpec((SUBLANES, n), lambda i: (i, 0)),
        out_shape=jax.ShapeDtypeStruct((rows, n), F32),
        compiler_params=_params("arbitrary"),
        name="cumsum",
    )(x)


def _softmax_update(s, v, m_prev, l_prev, acc_prev):
    m_new = jnp.maximum(m_prev, jnp.max(s, axis=1, keepdims=True))
    alpha = jnp.exp(m_prev - m_new)
    p = jnp.exp(s - m_new)
    l_new = alpha * l_prev + jnp.sum(p, axis=1, keepdims=True)
    acc_new = alpha * acc_prev + jnp.dot(p.astype(BF16), v, preferred_element_type=F32)
    return m_new, l_new, acc_new


FOX_TQ = 512
FOX_RQ = 128


def _fox_prompt_kernel(q_ref, k_ref, v_ref, f_ref, o_ref, m_scr, l_scr, acc_scr):
    tq = FOX_TQ
    qi = pl.program_id(2)
    m_scr[...] = jnp.full(m_scr.shape, -jnp.inf, F32)
    l_scr[...] = jnp.zeros(l_scr.shape, F32)
    acc_scr[...] = jnp.zeros(acc_scr.shape, F32)

    def block(ki, diagonal):
        keys = pl.ds(pl.multiple_of(ki * tq, tq), tq)
        k, v, f = k_ref[keys, :], v_ref[keys, :], f_ref[ki]
        subs = [slice(r * FOX_RQ, (r + 1) * FOX_RQ) for r in range(tq // FOX_RQ)]
        prev = [(m_scr[rs, :], l_scr[rs, :], acc_scr[rs, :]) for rs in subs]
        qk = [_nt(q_ref[rs, :], k) for rs in subs]
        new = []
        for r, rs in enumerate(subs):
            s = qk[r] * ATT_SCALE - f
            if diagonal:
                row = lax.broadcasted_iota(jnp.int32, s.shape, 0) + r * FOX_RQ
                col = lax.broadcasted_iota(jnp.int32, s.shape, 1)
                s = jnp.where(col <= row, s, -jnp.inf)
            new.append(_softmax_update(s, v, *prev[r]))
        for rs, (m, l, acc) in zip(subs, new):
            m_scr[rs, :], l_scr[rs, :], acc_scr[rs, :] = m, l, acc

    def below_diagonal(ki, c):
        block(ki, False)
        return c
    lax.fori_loop(0, qi, below_diagonal, 0)
    block(qi, True)
    o_ref[...] = (acc_scr[...] / l_scr[...]).astype(o_ref.dtype)


def _fox_prompt(h, f, batch, seq):
    tq = FOX_TQ
    nq = seq // tq
    rows = batch * seq
    kv = lambda part: pl.BlockSpec((None, seq, HEAD_DIM), lambda b, hh, qi: (part, b, hh))
    return pl.pallas_call(
        _fox_prompt_kernel,
        grid=(batch, N_HEADS, nq),
        in_specs=[pl.BlockSpec((None, tq, HEAD_DIM), lambda b, hh, qi: (0, b * nq + qi, hh)),
                  kv(1), kv(2),
                  pl.BlockSpec((None, nq, 1, tq), lambda b, hh, qi: (hh * batch + b, 0, 0, 0))],
        out_specs=pl.BlockSpec((tq, HEAD_DIM), lambda b, hh, qi: (b * nq + qi, hh)),
        out_shape=jax.ShapeDtypeStruct((rows, W_ATT), BF16),
        scratch_shapes=[pltpu.VMEM((tq, 1), F32), pltpu.VMEM((tq, 1), F32), pltpu.VMEM((tq, HEAD_DIM), F32)],
        compiler_params=_params("arbitrary", "arbitrary", "arbitrary"),
        name="fox_prompt",
    )(h, h, h, f.reshape(N_HEADS * batch, nq, 1, tq))


BAND_TQ = BAND_WINDOW


def _band_prompt_kernel(q_ref, ko_ref, kl_ref, vo_ref, vl_ref, bo_ref, bl_ref, o_ref):
    ko, kl, vo, vl = ko_ref[...], kl_ref[...], vo_ref[...], vl_ref[...]
    has_left = pl.program_id(2) > 0
    subs = [slice(r * FOX_RQ, (r + 1) * FOX_RQ) for r in range(BAND_TQ // FOX_RQ)]
    qk = [(_nt(q_ref[rs, :], ko), _nt(q_ref[rs, :], kl)) for rs in subs]
    for rs, (qo, ql) in zip(subs, qk):
        so = qo * ATT_SCALE + bo_ref[rs, :]
        sl = jnp.where(has_left, ql * ATT_SCALE + bl_ref[rs, :], -jnp.inf)
        m = jnp.maximum(jnp.max(so, axis=1, keepdims=True), jnp.max(sl, axis=1, keepdims=True))
        po, pp = jnp.exp(so - m), jnp.exp(sl - m)
        l = jnp.sum(po, axis=1, keepdims=True) + jnp.sum(pp, axis=1, keepdims=True)
        acc = (jnp.dot(po.astype(BF16), vo, preferred_element_type=F32)
               + jnp.dot(pp.astype(BF16), vl, preferred_element_type=F32))
        o_ref[rs, :] = (acc / l).astype(o_ref.dtype)


def _band_prompt(h, bias_own, bias_left, batch, seq):
    tq = BAND_TQ
    nq = seq // tq
    own = lambda part: pl.BlockSpec((None, tq, HEAD_DIM), lambda b, hh, qi: (part, b * nq + qi, hh))
    left = lambda part: pl.BlockSpec(
        (None, tq, HEAD_DIM), lambda b, hh, qi: (part, b * nq + jnp.maximum(qi - 1, 0), hh))
    bias = pl.BlockSpec((None, tq, tq), lambda b, hh, qi: (hh, 0, 0))
    return pl.pallas_call(
        _band_prompt_kernel,
        grid=(batch, N_HEADS, nq),
        in_specs=[own(3), own(4), left(4), own(5), left(5), bias, bias],
        out_specs=pl.BlockSpec((tq, HEAD_DIM), lambda b, hh, qi: (b * nq + qi, hh)),
        out_shape=jax.ShapeDtypeStruct((batch * seq, W_ATT), BF16),
        compiler_params=_params("arbitrary", "arbitrary", "arbitrary"),
        name="band_prompt",
    )(h, h, h, h, h, bias_own, bias_left)


def _skew_kernel(f_ref, o_ref):
    fb = jnp.broadcast_to(f_ref[...], o_ref.shape)
    o_ref[...] = pltpu.roll(fb, 0, 1, stride=1, stride_axis=0)


def _rel_bias_toeplitz(rel_bias, n_rows, n_cols, shift):
    hh = rel_bias.shape[0]
    w = -(-(n_rows + n_cols - 1) // LANES) * LANES
    m = np.arange(w)
    m = np.where(m < n_cols, m, m - w)
    f = rel_bias[:, np.clip(shift - m, -REL_MAX, REL_MAX) + REL_MAX].astype(F32)
    table = pl.pallas_call(
        _skew_kernel,
        grid=(hh,),
        in_specs=[pl.BlockSpec((None, 1, w), lambda h: (h, 0, 0))],
        out_specs=pl.BlockSpec((None, n_rows, w), lambda h: (h, 0, 0)),
        out_shape=jax.ShapeDtypeStruct((hh, n_rows, w), F32),
        compiler_params=_params("arbitrary"),
        name="rel_bias_table",
    )(f.reshape(hh, 1, w))
    return table[:, :, :n_cols]


def _band_prompt_bias(rel_bias):
    r = np.arange(BAND_TQ)[:, None]
    c = np.arange(BAND_TQ)[None, :]
    own = jnp.where((c // CHUNK <= r // CHUNK)[None], _rel_bias_toeplitz(rel_bias, BAND_TQ, BAND_TQ, 0), -jnp.inf)
    left = jnp.where((c // CHUNK >= r // CHUNK)[None],
                     _rel_bias_toeplitz(rel_bias, BAND_TQ, BAND_TQ, BAND_TQ), -jnp.inf)
    return own, left


def _fox_sample_kernel(q_ref, kc_hbm, vc_hbm, fc_ref, kn_ref, vn_ref, fn_ref, o_ref,
                       kbuf, vbuf, sem, m_scr, l_scr, acc_scr, *, nk, tk, t_new):
    b, ki = pl.program_id(0), pl.program_id(1)
    n = b * nk + ki
    slot = n % 2

    def cache_copies(bb, kk, sl_):
        start = pl.multiple_of(kk * tk, tk)
        cps = []
        for hh in range(N_HEADS):
            cps.append(pltpu.make_async_copy(kc_hbm.at[0, bb, pl.ds(start, tk), hh, :], kbuf.at[sl_, hh],
                                             sem.at[sl_, 0]))
            cps.append(pltpu.make_async_copy(vc_hbm.at[0, bb, pl.ds(start, tk), hh, :], vbuf.at[sl_, hh],
                                             sem.at[sl_, 1]))
        return cps

    @pl.when(n == 0)
    def _():
        for cp in cache_copies(b, ki, slot):
            cp.start()

    @pl.when(n + 1 < pl.num_programs(0) * nk)
    def _():
        last = ki == nk - 1
        for cp in cache_copies(jnp.where(last, b + 1, b), jnp.where(last, 0, ki + 1), 1 - slot):
            cp.start()

    @pl.when(ki == 0)
    def _():
        m_scr[...] = jnp.full(m_scr.shape, -jnp.inf, F32)
        l_scr[...] = jnp.zeros(l_scr.shape, F32)
        acc_scr[...] = jnp.zeros(acc_scr.shape, F32)

    for cp in cache_copies(b, ki, slot):
        cp.wait()

    for hh in range(N_HEADS):
        sl = slice(hh * HEAD_DIM, (hh + 1) * HEAD_DIM)
        s = _nt(q_ref[:, sl], kbuf[slot, hh].astype(BF16)) * ATT_SCALE - fc_ref[hh:hh + 1, :]
        m, l, acc = _softmax_update(s, vbuf[slot, hh].astype(BF16), m_scr[hh], l_scr[hh], acc_scr[hh])
        m_scr[hh], l_scr[hh], acc_scr[hh] = m, l, acc

    @pl.when(ki == nk - 1)
    def _():
        for hh in range(N_HEADS):
            sl = slice(hh * HEAD_DIM, (hh + 1) * HEAD_DIM)
            s = _nt(q_ref[:, sl], kn_ref[:, sl]) * ATT_SCALE - fn_ref[hh:hh + 1, 0:t_new]
            row = lax.broadcasted_iota(jnp.int32, s.shape, 0)
            col = lax.broadcasted_iota(jnp.int32, s.shape, 1)
            s = jnp.where(col <= row, s, -jnp.inf)
            m, l, acc = _softmax_update(s, vn_ref[:, sl], m_scr[hh], l_scr[hh], acc_scr[hh])
            o_ref[:, sl] = (acc / l).astype(o_ref.dtype)


def _fox_sample(h, cache_k, cache_v, f_cache, f_new, batch, t_new, tk=1024):
    past = cache_k.shape[2]
    nk = past // tk
    new = lambda part: pl.BlockSpec((None, t_new, W_ATT), lambda b, ki: (part, b, 0))
    cache = pl.BlockSpec(memory_space=pl.ANY)
    head_major = pltpu.VMEM((2, N_HEADS, tk, HEAD_DIM), cache_k.dtype)
    return pl.pallas_call(
        functools.partial(_fox_sample_kernel, nk=nk, tk=tk, t_new=t_new),
        grid=(batch, nk),
        in_specs=[new(0), cache, cache,
                  pl.BlockSpec((None, N_HEADS, tk), lambda b, ki: (b, 0, ki)),
                  new(1), new(2),
                  pl.BlockSpec((None, N_HEADS, LANES), lambda b, ki: (b, 0, 0))],
        out_specs=pl.BlockSpec((t_new, W_ATT), lambda b, ki: (b, 0)),
        out_shape=jax.ShapeDtypeStruct((batch * t_new, W_ATT), BF16),
        scratch_shapes=[head_major, head_major, pltpu.SemaphoreType.DMA((2, 2)),
                        pltpu.VMEM((N_HEADS, t_new, 1), F32), pltpu.VMEM((N_HEADS, t_new, 1), F32),
                        pltpu.VMEM((N_HEADS, t_new, HEAD_DIM), F32)],
        compiler_params=_params("arbitrary", "arbitrary"),
        name="fox_sample",
    )(h, cache_k, cache_v, f_cache, h, h, f_new)


def _band_sample_kernel(q_ref, kc_hbm, vc_hbm, kn_ref, vn_ref, bc_ref, bn_ref, o_ref, kbuf, vbuf, sem):
    b = pl.program_id(0)
    slot = b % 2

    def cache_copies(bb, sl_):
        cps = []
        for hh in range(N_HEADS):
            cps.append(pltpu.make_async_copy(kc_hbm.at[0, bb, :, hh, :], kbuf.at[sl_, hh], sem.at[sl_, 0]))
            cps.append(pltpu.make_async_copy(vc_hbm.at[0, bb, :, hh, :], vbuf.at[sl_, hh], sem.at[sl_, 1]))
        return cps

    @pl.when(b == 0)
    def _():
        for cp in cache_copies(b, slot):
            cp.start()

    @pl.when(b + 1 < pl.num_programs(0))
    def _():
        for cp in cache_copies(b + 1, 1 - slot):
            cp.start()

    for cp in cache_copies(b, slot):
        cp.wait()

    for hh in range(N_HEADS):
        sl = slice(hh * HEAD_DIM, (hh + 1) * HEAD_DIM)
        q = q_ref[:, sl]
        sc = _nt(q, kbuf[slot, hh].astype(BF16)) * ATT_SCALE + bc_ref[hh]
        sn = _nt(q, kn_ref[:, sl]) * ATT_SCALE + bn_ref[hh]
        m = jnp.maximum(jnp.max(sc, axis=1, keepdims=True), jnp.max(sn, axis=1, keepdims=True))
        pc, pn = jnp.exp(sc - m), jnp.exp(sn - m)
        l = jnp.sum(pc, axis=1, keepdims=True) + jnp.sum(pn, axis=1, keepdims=True)
        acc = (jnp.dot(pc.astype(BF16), vbuf[slot, hh].astype(BF16), preferred_element_type=F32)
               + jnp.dot(pn.astype(BF16), vn_ref[:, sl], preferred_element_type=F32))
        o_ref[:, sl] = (acc / l).astype(o_ref.dtype)


def _band_sample(h, cache_k, cache_v, bias_c, bias_n, batch, t_new):
    win = cache_k.shape[2]
    new = lambda part: pl.BlockSpec((None, t_new, W_ATT), lambda b: (part, b, 0))
    cache = pl.BlockSpec(memory_space=pl.ANY)
    head_major = pltpu.VMEM((2, N_HEADS, win, HEAD_DIM), cache_k.dtype)
    return pl.pallas_call(
        _band_sample_kernel,
        grid=(batch,),
        in_specs=[new(3), cache, cache, new(4), new(5),
                  pl.BlockSpec((N_HEADS, t_new, win), lambda b: (0, 0, 0)),
                  pl.BlockSpec((N_HEADS, t_new, t_new), lambda b: (0, 0, 0))],
        out_specs=pl.BlockSpec((t_new, W_ATT), lambda b: (b, 0)),
        out_shape=jax.ShapeDtypeStruct((batch * t_new, W_ATT), BF16),
        scratch_shapes=[head_major, head_major, pltpu.SemaphoreType.DMA((2, 2))],
        compiler_params=_params("arbitrary"),
        name="band_sample",
    )(h, cache_k, cache_v, h, h, bias_c, bias_n)


def _band_sample_bias(rel_bias, win, t_new):
    b = _rel_bias_toeplitz(rel_bias, t_new, win + t_new, win)
    return b[:, :, :win], b[:, :, win:]


def _merge_kernel(ya_ref, yb_ref, ga0_ref, ga1_ref, gb0_ref, gb1_ref, x_ref, g1_ref,
                  wf_ref, wb_ref, wo_ref, o_ref):
    a = jnp.dot(ya_ref[...], wf_ref[...], preferred_element_type=F32)
    b = jnp.dot(yb_ref[...], wb_ref[...], preferred_element_type=F32)
    sig = lambda z: 1.0 / (1.0 + jnp.exp(-z))
    half = D_MODEL // 2
    m0 = (sig(ga0_ref[...]) * a[:, :half] + sig(gb0_ref[...]) * b[:, :half]).astype(BF16)
    m1 = (sig(ga1_ref[...]) * a[:, half:] + sig(gb1_ref[...]) * b[:, half:]).astype(BF16)
    y = (jnp.dot(m0, wo_ref[:half, :], preferred_element_type=F32)
         + jnp.dot(m1, wo_ref[half:, :], preferred_element_type=F32))
    o_ref[...] = x_ref[...] + g1_ref[...] * y


def _merge(ya, yb, h, x2d, g13, mod_spec, wf, wb, wo, tm):
    rows = x2d.shape[0]
    part = lambda p: pl.BlockSpec((None, tm, W_ATT), lambda i: (p, i, 0))
    const = lambda shape: pl.BlockSpec(shape, lambda i: (0, 0), pipeline_mode=pl.Buffered(1))
    return pl.pallas_call(
        _merge_kernel,
        grid=(rows // tm,),
        in_specs=[pl.BlockSpec((tm, W_ATT), lambda i: (i, 0)), pl.BlockSpec((tm, W_ATT), lambda i: (i, 0)),
                  part(0), part(1), part(2), part(3),
                  pl.BlockSpec((tm, D_MODEL), lambda i: (i, 0)),
                  mod_spec,
                  const((W_ATT, D_MODEL)), const((W_ATT, D_MODEL)), const((D_MODEL, D_MODEL))],
        out_specs=pl.BlockSpec((tm, D_MODEL), lambda i: (i, 0)),
        out_shape=jax.ShapeDtypeStruct((rows, D_MODEL), F32),
        compiler_params=_params("arbitrary"),
        name="merge",
    )(ya, yb, h, h, h, h, x2d, g13, wf, wb, wo)


def _peer_query_kernel(x_ref, gain_ref, sc_ref, sh_ref, w_ref, q_ref, xf_ref, xm_scr):
    @pl.when(pl.program_id(1) == 0)
    def _():
        xf = _norm_mod(x_ref[...], gain_ref[...], sc_ref[...], sh_ref[...])
        xf_ref[...] = xf
        xm_scr[...] = xf.astype(BF16)

    q_ref[...] = jnp.dot(xm_scr[...], w_ref[...], preferred_element_type=F32)


def _peer_query(x2d, gain, sc3, sh3, mod_spec, wq, tm):
    rows = x2d.shape[0]
    tn = wq.shape[1]
    return pl.pallas_call(
        _peer_query_kernel,
        grid=(rows // tm, wq.shape[1] // tn),
        in_specs=[pl.BlockSpec((tm, D_MODEL), lambda i, j: (i, 0)),
                  pl.BlockSpec((1, D_MODEL), lambda i, j: (0, 0)),
                  mod_spec, mod_spec,
                  pl.BlockSpec((D_MODEL, tn), lambda i, j: (0, j), pipeline_mode=pl.Buffered(1))],
        out_specs=[pl.BlockSpec((tm, tn), lambda i, j: (i, j)),
                   pl.BlockSpec((tm, D_MODEL), lambda i, j: (i, 0))],
        out_shape=[jax.ShapeDtypeStruct((rows, wq.shape[1]), F32),
                   jax.ShapeDtypeStruct((rows, D_MODEL), F32)],
        scratch_shapes=[pltpu.VMEM((tm, D_MODEL), BF16)],
        compiler_params=_params("arbitrary", "arbitrary"),
        name="peer_query",
    )(x2d, gain, sc3, sh3, wq)


def _top_k_rows(s, k, ids=None):
    iota = lax.broadcasted_iota(jnp.int32, s.shape, 0) if ids is None else ids
    n = jnp.iinfo(jnp.int32).max
    vals, idxs = [], []
    for _ in range(k):
        m = jnp.max(s, axis=0, keepdims=True)
        i = jnp.min(jnp.where(s == m, iota, n), axis=0, keepdims=True)
        vals.append(m)
        idxs.append(i)
        s = jnp.where(iota == i, -jnp.inf, s)
    return jnp.concatenate(vals, axis=0), jnp.concatenate(idxs, axis=0)


def _pair_candidates(s1, s2, k):
    assert k // 2 == SUBLANES
    row = lax.broadcasted_iota(jnp.int32, (SUBLANES, s1.shape[1]), 0)
    cands, ids = [], []
    for a in range(k // 2):
        nb = k // (a + 1)
        for b0 in range(0, nb, SUBLANES):
            piece = s1[a:a + 1, :] + s2[b0:b0 + SUBLANES, :]
            if nb - b0 < SUBLANES:
                piece = jnp.where(row < nb - b0, piece, -jnp.inf)
            cands.append(piece)
            ids.append(a * k + b0 + row)
    cands.append(s1[k // 2:k, :] + s2[0:1, :])
    ids.append((k // 2 + row) * k)
    return jnp.concatenate(cands, axis=0), jnp.concatenate(ids, axis=0)


def _select_rows(table, sel):
    out = jnp.zeros(sel.shape, table.dtype)
    for r in range(table.shape[0]):
        out = jnp.where(sel == r, table[r:r + 1, :], out)
    return out


def _route_kernel(q_ref, sk_ref, e_ref, g_ref):
    k = PEER_TOPK
    e_all, g_all = [], []
    for hh in range(PEER_HEADS):
        halves = []
        for c in range(2):
            col = (2 * hh + c) * PEER_HALF
            qh = q_ref[:, col:col + PEER_HALF].astype(BF16)
            halves.append(_top_k_rows(_nt(sk_ref[hh, c].astype(BF16), qh), k))
        (s1, i1), (s2, i2) = halves
        cand, cand_ids = _pair_candidates(s1, s2, k)
        top_s, top_c = _top_k_rows(cand, k, cand_ids)
        e_all.append(_select_rows(i1, top_c >> (k.bit_length() - 1)) * N_KEYS + _select_rows(i2, top_c & (k - 1)))
        p = jnp.exp(top_s - top_s[0:1, :])
        g_all.append(p / jnp.sum(p, axis=0, keepdims=True))
    e_ref[...] = jnp.concatenate(e_all, axis=0).T
    g_ref[...] = jnp.concatenate(g_all, axis=0).T


def _route(qp, sub_keys, tt):
    rows = qp.shape[0]
    return pl.pallas_call(
        _route_kernel,
        grid=(rows // tt,),
        in_specs=[pl.BlockSpec((tt, qp.shape[1]), lambda i: (i, 0)),
                  pl.BlockSpec(sub_keys.shape, lambda i: (0, 0, 0, 0))],
        out_specs=[pl.BlockSpec((tt, PEER_SLOTS), lambda i: (i, 0)),
                   pl.BlockSpec((tt, PEER_SLOTS), lambda i: (i, 0))],
        out_shape=[jax.ShapeDtypeStruct((rows, PEER_SLOTS), jnp.int32),
                   jax.ShapeDtypeStruct((rows, PEER_SLOTS), F32)],
        compiler_params=_params("arbitrary"),
        name="peer_route",
    )(qp, sub_keys)


PEER_TOK = 4
PEER_NBUF = 4
PEER_AHEAD = 2
UV_ROWS = 2 * ROW_TILES
PEER_GROUP = SUBLANES
PEER_NGROUP = PEER_SLOTS // PEER_GROUP


def _peer_kernel(idx_ref, g_ref, xf_ref, x1_ref, g2_ref, nf_ref, uv_ref, y_ref,
                 buf0, buf1, buf2, buf3, p_scr, wb_scr, xf3_scr, o3_scr, sem, *, tb):
    bufs = (buf0, buf1, buf2, buf3)
    n_stage = tb // PEER_TOK
    step = pl.program_id(0)

    def issue(tok, b, k, j):
        e = idx_ref[0, tok * PEER_SLOTS + j]
        pltpu.make_async_copy(uv_ref.at[e], bufs[b].at[k, j], sem.at[b, k]).start()

    def wait(b, k):
        pltpu.make_async_copy(uv_ref.at[pl.ds(0, PEER_SLOTS)], bufs[b].at[k], sem.at[b, k]).wait()

    @pl.when(step == 0)
    def _():
        def prologue(jo, c):
            for ji in range(PEER_GROUP):
                for b in range(PEER_AHEAD):
                    for k in range(PEER_TOK):
                        issue(b * PEER_TOK + k, b, k, jo * PEER_GROUP + ji)
            return c
        lax.fori_loop(0, PEER_NGROUP, prologue, 0)

    for c in range(ROW_TILES):
        xf3_scr[:, c, :] = xf_ref[:, c * LANES:(c + 1) * LANES]

    def stage(s, b):
        buf, nb = bufs[b], (b + PEER_AHEAD) % PEER_NBUF
        t0 = s * PEER_TOK
        tn = t0 + PEER_AHEAD * PEER_TOK
        for k in range(PEER_TOK):
            wait(b, k)
        xs = [(xf3_scr[t0 + k, 0:SUBLANES, :], xf3_scr[t0 + k, SUBLANES:ROW_TILES, :])
              for k in range(PEER_TOK)]

        def dot_body(jo, c):
            for ji in range(PEER_GROUP):
                j = jo * PEER_GROUP + ji
                issue(tn, nb, 0, j)
                issue(tn + 1, nb, 1, j)
                row = pl.multiple_of(j * SUBLANES, SUBLANES)
                for k in range(PEER_TOK):
                    u = buf[k, j, 0:ROW_TILES, :].astype(F32)
                    p = u[0:SUBLANES] * xs[k][0] + u[SUBLANES:ROW_TILES] * xs[k][1]
                    p_scr[k, pl.ds(row, SUBLANES), :] = p
            return c
        lax.fori_loop(0, PEER_NGROUP, dot_body, 0)

        for k in range(PEER_TOK):
            ps = p_scr[k, pl.ds(0, PEER_SLOTS, stride=SUBLANES), :]
            for r in range(1, SUBLANES):
                ps = ps + p_scr[k, pl.ds(r, PEER_SLOTS, stride=SUBLANES), :]
            h = jnp.sum(ps.T, axis=0, keepdims=True)
            w = g_ref[pl.ds(t0 + k, 1), :] * (0.5 * h * (1.0 + lax.erf(h * (2.0 ** -0.5))))
            wb_scr[k] = jnp.broadcast_to(w, (PEER_SLOTS, PEER_SLOTS)).T

        def acc_body(jo, accs):
            accs = list(accs)
            for ji in range(PEER_GROUP):
                j = jo * PEER_GROUP + ji
                issue(tn + 2, nb, 2, j)
                issue(tn + 3, nb, 3, j)
                for k in range(PEER_TOK):
                    wv = wb_scr[k, pl.ds(j, 1), :]
                    v = buf[k, j, ROW_TILES:UV_ROWS, :].astype(F32)
                    accs[2 * k] = accs[2 * k] + v[0:SUBLANES] * wv
                    accs[2 * k + 1] = accs[2 * k + 1] + v[SUBLANES:ROW_TILES] * wv
            return tuple(accs)
        zero = jnp.zeros((SUBLANES, LANES), F32)
        accs = lax.fori_loop(0, PEER_NGROUP, acc_body, (zero,) * (2 * PEER_TOK))
        for k in range(PEER_TOK):
            o3_scr[t0 + k, 0:SUBLANES, :] = accs[2 * k]
            o3_scr[t0 + k, SUBLANES:ROW_TILES, :] = accs[2 * k + 1]

    def rotation(q, c):
        for b in range(PEER_NBUF):
            stage(q * PEER_NBUF + b, b)
        return c
    lax.fori_loop(0, n_stage // PEER_NBUF, rotation, 0)

    @pl.when(step == pl.num_programs(0) - 1)
    def _():
        for b in range(PEER_AHEAD):
            for k in range(PEER_TOK):
                wait(b, k)

    ss = jnp.zeros((tb, 1), F32)
    for c in range(ROW_TILES):
        sl = slice(c * LANES, (c + 1) * LANES)
        z = x1_ref[:, sl] + g2_ref[:, sl] * o3_scr[:, c, :]
        y_ref[:, sl] = z
        ss = ss + jnp.sum(z * z, axis=1, keepdims=True)
    inv = lax.rsqrt(ss * (1.0 / D_MODEL) + EPS)
    for c in range(ROW_TILES):
        sl = slice(c * LANES, (c + 1) * LANES)
        y_ref[:, sl] = y_ref[:, sl] * inv * nf_ref[:, sl]


def _peer(idx, gate, xf, x1, g23, g2_spec, nf, uv, tb):
    rows = xf.shape[0]
    nblk = rows // tb
    tok = lambda: pl.BlockSpec((tb, D_MODEL), lambda i: (i, 0))
    n_ahead = PEER_AHEAD * PEER_TOK * PEER_SLOTS
    idx2 = idx.reshape(nblk, tb * PEER_SLOTS)
    nxt = jnp.concatenate([idx2[1:, :n_ahead], jnp.zeros((1, n_ahead), idx.dtype)], axis=0)
    idx_ext = jnp.concatenate([idx2, nxt], axis=1).reshape(nblk, 1, tb * PEER_SLOTS + n_ahead)
    gather_buf = pltpu.VMEM((PEER_TOK, PEER_SLOTS, UV_ROWS, LANES), uv.dtype)
    return pl.pallas_call(
        functools.partial(_peer_kernel, tb=tb),
        grid=(nblk,),
        in_specs=[pl.BlockSpec((None, 1, tb * PEER_SLOTS + n_ahead), lambda i: (i, 0, 0),
                               memory_space=pltpu.SMEM),
                  pl.BlockSpec((tb, PEER_SLOTS), lambda i: (i, 0)),
                  tok(), tok(), g2_spec,
                  pl.BlockSpec((1, D_MODEL), lambda i: (0, 0)),
                  pl.BlockSpec(memory_space=pl.ANY)],
        out_specs=tok(),
        out_shape=jax.ShapeDtypeStruct((rows, D_MODEL), F32),
        scratch_shapes=[gather_buf] * PEER_NBUF + [
                        pltpu.VMEM((PEER_TOK, PEER_SLOTS * SUBLANES, LANES), F32),
                        pltpu.VMEM((PEER_TOK, PEER_SLOTS, LANES), F32),
                        pltpu.VMEM((tb, ROW_TILES, LANES), F32),
                        pltpu.VMEM((tb, ROW_TILES, LANES), F32),
                        pltpu.SemaphoreType.DMA((PEER_NBUF, PEER_TOK))],
        compiler_params=_params("arbitrary"),
        name="peer_experts",
    )(idx_ext, gate, xf, x1, g23, nf, uv)


def _stream(x, mods, weights, attend, tm_in, tm, tb):
    batch, seq, _ = x.shape
    rows = batch * seq
    tm, tm_in = min(tm, rows), min(tm_in, rows)
    sh1, sc1, g1, sh2, sc2, g2 = mods
    x2d = x.reshape(rows, D_MODEL)

    def modulation(tm_):
        if seq >= tm_:
            per = seq // tm_
            return (lambda m: m.reshape(batch, 1, D_MODEL),
                    pl.BlockSpec((None, 1, D_MODEL), lambda i, j: (i // per, 0, 0)))
        return (lambda m: jnp.repeat(m, seq, axis=0).reshape(rows // tm_, tm_, D_MODEL),
                pl.BlockSpec((None, tm_, D_MODEL), lambda i, j: (i, 0, 0)))

    mod3_in, mod_spec_in = modulation(tm_in)
    mod3, mod_spec2 = modulation(tm)
    hb, hg, logf_t, *kv = _inproj(x2d, weights["norm_mix"], mod3_in(sc1), mod3_in(sh1), mod_spec_in,
                                  weights["w_cat"], weights["wff_t"], weights["bf_col"], tm_in)
    ya, yb = attend(hb, logf_t)
    tm2 = min(tm, 256)
    if seq >= tm2:
        per2 = seq // tm2
        g13 = g1.reshape(batch, 1, D_MODEL)
        g1_spec = pl.BlockSpec((None, 1, D_MODEL), lambda i: (i // per2, 0, 0))
    else:
        g13 = jnp.repeat(g1, seq, axis=0).reshape(rows // tm2, tm2, D_MODEL)
        g1_spec = pl.BlockSpec((None, tm2, D_MODEL), lambda i: (i, 0, 0))
    x1 = _merge(ya, yb, hg, x2d, g13, g1_spec, weights["wf"], weights["wb"], weights["wo"], tm2)
    qp, xf = _peer_query(x1, weights["norm_ffn"], mod3(sc2), mod3(sh2), mod_spec2, weights["wq"], tm)
    idx, gate = _route(qp, weights["sub_keys"], 256)
    per_tb = seq // tb
    g2_spec = pl.BlockSpec((None, 1, D_MODEL), lambda i: (i // per_tb, 0, 0))
    y = _peer(idx, gate, xf, x1, g2.reshape(batch, 1, D_MODEL), g2_spec, weights["nf"], weights["uv"], tb)
    return y.reshape(batch, seq, D_MODEL), kv, logf_t


def kernel(x_prompt, x_sample, c_prompt, c_sample, cache_fox_k, cache_fox_v, cache_fox_logf, cache_band_k, cache_band_v, w_ada, b_ada, norm_mix, norm_ffn, w_in, b_forget, rel_bias, w_branch_fox, w_branch_band, w_out, w_query, sub_keys, expert_u, expert_v, norm_final):
    bp, sp, _ = x_prompt.shape
    bs, ts, _ = x_sample.shape
    past = cache_fox_k.shape[2]
    win = cache_band_k.shape[2]
    n_exp = expert_u.shape[1]

    w = w_in[0]
    o_ff = 3 * W_ATT
    weights = {
        "w_cat": jnp.concatenate([w[:, :o_ff], w[:, o_ff + N_HEADS:]], axis=1).astype(BF16),
        "wff_t": w[:, o_ff:o_ff + N_HEADS].T,
        "bf_col": b_forget[0].reshape(N_HEADS, 1),
        "norm_mix": norm_mix[0].reshape(1, D_MODEL),
        "norm_ffn": norm_ffn[0].reshape(1, D_MODEL),
        "wf": w_branch_fox[0].astype(BF16),
        "wb": w_branch_band[0].astype(BF16),
        "wo": w_out[0].astype(BF16),
        "wq": w_query[0].astype(BF16),
        "sub_keys": sub_keys[0],
        "nf": norm_final.reshape(1, D_MODEL),
        "uv": jnp.concatenate([expert_u[0].astype(BF16).reshape(n_exp, ROW_TILES, LANES),
                               expert_v[0].astype(BF16).reshape(n_exp, ROW_TILES, LANES)], axis=1),
    }
    rb = rel_bias[0]
    bias_own, bias_left = _band_prompt_bias(rb)
    bias_c, bias_n = _band_sample_bias(rb, win, ts)

    ada = _adaln(jnp.concatenate([c_prompt, c_sample], axis=0), w_ada[0], b_ada[0])
    mods_p = [ada[:bp, i * D_MODEL:(i + 1) * D_MODEL] for i in range(6)]
    mods_s = [ada[bp:, i * D_MODEL:(i + 1) * D_MODEL] for i in range(6)]

    def attend_prompt(h, logf_t):
        f = _cumsum_rows(logf_t.reshape(N_HEADS * bp, sp))
        ya = _fox_prompt(h, f, bp, sp)
        yb = _band_prompt(h, bias_own, bias_left, bp, sp)
        return ya, yb

    y_prompt, kv_p, logf_p = _stream(x_prompt, mods_p, weights, attend_prompt, tm_in=1024, tm=512, tb=64)


    def attend_sample(h, logf_t):
        lf_new = logf_t.reshape(N_HEADS, bs, ts).transpose(1, 0, 2)
        lf_all = jnp.concatenate([cache_fox_logf[0].transpose(0, 2, 1), lf_new], axis=2)
        n_pad = -(past + ts) % CUMSUM_CHUNK
        lf_all = jnp.pad(lf_all, ((0, 0), (0, 0), (0, n_pad)))
        f = _cumsum_rows(lf_all.reshape(bs * N_HEADS, -1)).reshape(bs, N_HEADS, -1)
        ya = _fox_sample(h, cache_fox_k, cache_fox_v, f[:, :, :past], f[:, :, past:past + LANES], bs, ts)
        yb = _band_sample(h, cache_band_k, cache_band_v, bias_c, bias_n, bs, ts)
        return ya, yb

    y_sample, kv_s, logf_s = _stream(x_sample, mods_s, weights, attend_sample, tm_in=512, tm=512, tb=ts)

    heads = lambda part, b, t: part.reshape(1, b, t, N_HEADS, HEAD_DIM)
    logf_out = lambda lt, b, t: lt.T.reshape(1, b, t, N_HEADS)
    w_keep = min(BAND_WINDOW, sp)
    return (y_prompt, y_sample,
            heads(kv_p[0], bp, sp), heads(kv_p[1], bp, sp), logf_out(logf_p, bp, sp),
            heads(kv_p[2], bp, sp)[:, :, sp - w_keep:], heads(kv_p[3], bp, sp)[:, :, sp - w_keep:],
            heads(kv_s[0], bs, ts), heads(kv_s[1], bs, ts), logf_out(logf_s, bs, ts),
            heads(kv_s[2], bs, ts), heads(kv_s[3], bs, ts))
```
